```python
import jax, jax.numpy as jnp
from jax import lax
import numpy as np

D_MODEL = 1024
BATCH = 16
SEQ = 4096
DEPTH = 2

N_EVEN = (DEPTH + 1) // 2
N_ODD = DEPTH // 2
D_FF = 4 * D_MODEL
NORM_EPS = 1e-6
CHUNK = 128

RWKV_WIDTH = D_MODEL // 2
RWKV_HEAD_DIM = 64
RWKV_HEADS = RWKV_WIDTH // RWKV_HEAD_DIM
DECAY_LORA = 64
ICL_LORA = 64
GATE_LORA = 128
RWKV_GN_EPS = 64e-5
RWKV_COLS = 3 * RWKV_WIDTH + DECAY_LORA + ICL_LORA + GATE_LORA

RET_WIDTH = D_MODEL - RWKV_WIDTH
RET_HEADS = 4
RET_HEAD_DIM = RET_WIDTH // RET_HEADS
ROPE_BASE = 10000.0
RET_COLS = 4 * RET_WIDTH
AB_COLS = RWKV_COLS + RET_COLS

MLSTM_HEADS = 8
MLSTM_QK_DIM = D_MODEL // 2 // MLSTM_HEADS
MLSTM_V_DIM = D_MODEL // MLSTM_HEADS
MLSTM_CONV = 4
GATE_SOFTCAP = 15.0
MLSTM_QK_COLS = 2 * MLSTM_HEADS * MLSTM_QK_DIM
C_COLS = MLSTM_QK_COLS + 2 * D_MODEL + 2 * MLSTM_HEADS

kernel_name = 'hybrid_rwkv7_retention_mlstm_trunk'

F32 = jnp.float32


def rms_norm(x, g):
    xf = x.astype(F32)
    y = xf * lax.rsqrt(jnp.mean(xf * xf, axis=-1, keepdims=True) + NORM_EPS)
    return (y * g.astype(F32)).astype(x.dtype)


def token_shift(x):
    return jnp.pad(x, ((0, 0), (1, 0), (0, 0)))[:, :-1]


def split_heads(x, n_heads):
    return x.reshape(x.shape[:-1] + (n_heads, x.shape[-1] // n_heads))


def rwkv7_scan(r, decay, k, v, a, b):
    bsz, _, n_heads, n = r.shape
    xs = tuple(jnp.moveaxis(t, 1, 0) for t in (r, decay, k, v, a, b))

    def step(state, inp):
        r_t, w_t, k_t, v_t, a_t, b_t = inp
        sa = jnp.einsum('bhij,bhj->bhi', state, a_t)
        state = (state * w_t[:, :, None, :] + sa[..., None] * b_t[:, :, None, :]
                 + v_t[..., None] * k_t[:, :, None, :])
        return state, jnp.einsum('bhij,bhj->bhi', state, r_t)

    s0 = jnp.zeros((bsz, n_heads, n, n), F32)
    _, y = lax.scan(step, s0, xs)
    return jnp.moveaxis(y, 0, 1)


def rwkv7_group(p, mu, w0, w_up, a0, a_up, g_up, k_k, k_a, r_k, ln_w, ln_b):
    bsz, seq, _ = p.shape
    p = p + (token_shift(p) - p) * mu
    cuts = [RWKV_WIDTH, 2 * RWKV_WIDTH, 3 * RWKV_WIDTH,
            3 * RWKV_WIDTH + DECAY_LORA, 3 * RWKV_WIDTH + DECAY_LORA + ICL_LORA]
    r, k, v, w_lo, a_lo, g_lo = jnp.split(p, cuts, axis=-1)
    w = -jax.nn.softplus(-(w0 + jnp.tanh(w_lo) @ w_up)) - 0.5
    decay = jnp.exp(-jnp.exp(w.astype(F32)))
    a = jax.nn.sigmoid(a0 + a_lo @ a_up)
    g = jax.nn.sigmoid(g_lo) @ g_up
    kk = split_heads((k * k_k).astype(F32), RWKV_HEADS)
    kk = kk * lax.rsqrt(jnp.maximum(jnp.sum(kk * kk, -1, keepdims=True), 1e-24))
    k = k * (1.0 + (a - 1.0) * k_a)
    rh, kh, vh, ah = [split_heads(t.astype(F32), RWKV_HEADS) for t in (r, k, v, a)]
    dh = split_heads(decay, RWKV_HEADS)
    y = rwkv7_scan(rh, dh, kh, vh, -kk, kk * ah)
    mean = jnp.mean(y, -1, keepdims=True)
    var = jnp.mean(jnp.square(y - mean), -1, keepdims=True)
    y = (y - mean) * lax.rsqrt(var + RWKV_GN_EPS)
    y = y.reshape(bsz, seq, RWKV_WIDTH) * ln_w + ln_b
    bonus = jnp.sum(rh * kh * r_k, -1, keepdims=True) * vh
    y = (y + bonus.reshape(bsz, seq, RWKV_WIDTH)) * g
    return y.astype(p.dtype)


def rotary(x, pos):
    d = x.shape[-1]
    inv = ROPE_BASE ** (-jnp.arange(0, d, 2, dtype=F32) / d)
    ang = pos[:, None] * inv[None, :]
    cos = jnp.cos(ang)[None, :, None, :]
    sin = jnp.sin(ang)[None, :, None, :]
    x1, x2 = x[..., : d // 2], x[..., d // 2:]
    return jnp.concatenate([x1 * cos - x2 * sin, x1 * sin + x2 * cos], axis=-1)


def chunk_bthd(x):
    bsz, seq, h, d = x.shape
    return x.reshape(bsz, seq // CHUNK, CHUNK, h, d).transpose(1, 0, 3, 2, 4)


def unchunk_bthd(x):
    nc, bsz, h, l, d = x.shape
    return x.transpose(1, 0, 3, 2, 4).reshape(bsz, nc * l, h, d)


def retention_chunkwise(q, k, v):
    bsz, _, n_heads, d = q.shape
    log_gamma = jnp.log1p(-jnp.exp2(-5.0 - jnp.arange(n_heads, dtype=F32)))
    idx = jnp.arange(CHUNK, dtype=F32)
    rel = idx[:, None] - idx[None, :]
    causal = rel >= 0
    decay_mat = jnp.where(causal[None], jnp.exp(log_gamma[:, None, None] * jnp.where(causal, rel, 0.0)[None]), 0.0)
    q_decay = jnp.exp(log_gamma[:, None] * (idx + 1.0)[None])
    k_decay = jnp.exp(log_gamma[:, None] * (CHUNK - 1.0 - idx)[None])
    chunk_decay = jnp.exp(log_gamma * CHUNK)

    def step(state, inp):
        qc, kc, vc = inp
        s = jnp.einsum('bhld,bhmd->bhlm', qc, kc) * decay_mat
        o = (jnp.einsum('bhlm,bhme->bhle', s, vc)
             + jnp.einsum('bhld,bhde->bhle', qc, state) * q_decay[None, :, :, None])
        state = (state * chunk_decay[None, :, None, None]
                 + jnp.einsum('bhmd,bhme->bhde', kc * k_decay[None, :, :, None], vc))
        return state, o

    s0 = jnp.zeros((bsz, n_heads, d, d), F32)
    _, o = lax.scan(step, s0, (chunk_bthd(q), chunk_bthd(k), chunk_bthd(v)))
    return unchunk_bthd(o)


def retention_group(p, pos):
    bsz, seq, _ = p.shape
    q, k, v, g = jnp.split(p, 4, axis=-1)
    q = rotary(split_heads(q.astype(F32), RET_HEADS), pos)
    k = rotary(split_heads(k.astype(F32), RET_HEADS), pos) * (RET_HEAD_DIM ** -0.5)
    v = split_heads(v.astype(F32), RET_HEADS)
    o = retention_chunkwise(q, k, v)
    o = o * lax.rsqrt(jnp.mean(o * o, -1, keepdims=True) + NORM_EPS)
    y = o.reshape(bsz, seq, RET_WIDTH) * jax.nn.silu(g.astype(F32))
    return y.astype(p.dtype)


def causal_depthwise_conv(x, w, b):
    n_taps = w.shape[0]
    seq = x.shape[1]
    xp = jnp.pad(x, ((0, 0), (n_taps - 1, 0), (0, 0)))
    return sum(xp[:, j:j + seq] * w[j] for j in range(n_taps)) + b


def to_chunks(x):
    bsz, h, seq = x.shape[:3]
    x = x.reshape((bsz, h, seq // CHUNK, CHUNK) + x.shape[3:])
    return jnp.moveaxis(x, 2, 0)


def from_chunks(x):
    x = jnp.moveaxis(x, 0, 2)
    return x.reshape(x.shape[:2] + (x.shape[2] * x.shape[3],) + x.shape[4:])


def mlstm_chunkwise(q, k, v, log_i, log_f):
    bsz, n_heads, _, dk = q.shape
    dv = v.shape[-1]
    causal = jnp.tril(jnp.ones((CHUNK, CHUNK), bool))

    def step(carry, inp):
        c_st, n_st, m_st = carry
        qc, kc, vc, li, lf = inp
        b = jnp.cumsum(lf, axis=-1)
        log_d = jnp.where(causal, b[..., :, None] - b[..., None, :] + li[..., None, :], -jnp.inf)
        log_inter = b + m_st[..., None]
        m_t = jnp.maximum(jnp.max(log_d, axis=-1), log_inter)
        s = jnp.einsum('bhld,bhsd->bhls', qc, kc) * jnp.exp(log_d - m_t[..., None])
        inter = jnp.exp(log_inter - m_t)
        num = (jnp.einsum('bhls,bhse->bhle', s, vc)
               + inter[..., None] * jnp.einsum('bhld,bhde->bhle', qc, c_st))
        den = jnp.sum(s, axis=-1) + inter * jnp.einsum('bhld,bhd->bhl', qc, n_st)
        h = num / jnp.maximum(jnp.abs(den), jnp.exp(-m_t))[..., None]
        b_end = b[..., -1]
        log_w = b_end[..., None] - b + li
        m_new = jnp.maximum(b_end + m_st, jnp.max(log_w, axis=-1))
        kw = kc * jnp.exp(log_w - m_new[..., None])[..., None]
        carry_scale = jnp.exp(b_end + m_st - m_new)
        c_st = carry_scale[..., None, None] * c_st + jnp.einsum('bhsd,bhse->bhde', kw, vc)
        n_st = carry_scale[..., None] * n_st + jnp.sum(kw, axis=2)
        return (c_st, n_st, m_new), h

    carry0 = (jnp.zeros((bsz, n_heads, dk, dv), F32),
              jnp.zeros((bsz, n_heads, dk), F32),
              jnp.zeros((bsz, n_heads), F32))
    xs = (to_chunks(q), to_chunks(k), to_chunks(v), to_chunks(log_i), to_chunks(log_f))
    _, h = lax.scan(step, carry0, xs)
    return from_chunks(h)


def softcap(x):
    return GATE_SOFTCAP * jnp.tanh(x / GATE_SOFTCAP)


def mlstm_group(p, conv_w, conv_b, i_bias, f_bias, norm_w):
    bsz, seq, _ = p.shape
    cuts = [MLSTM_QK_COLS, MLSTM_QK_COLS + D_MODEL, MLSTM_QK_COLS + 2 * D_MODEL,
            MLSTM_QK_COLS + 2 * D_MODEL + MLSTM_HEADS]
    qk, v, o, i_pre, f_pre = jnp.split(p, cuts, axis=-1)
    qk = jax.nn.silu(causal_depthwise_conv(qk, conv_w, conv_b))
    q, k = jnp.split(qk.astype(F32), 2, axis=-1)
    q = split_heads(q, MLSTM_HEADS).transpose(0, 2, 1, 3)
    k = split_heads(k, MLSTM_HEADS).transpose(0, 2, 1, 3) * (MLSTM_QK_DIM ** -0.5)
    vh = split_heads(v.astype(F32), MLSTM_HEADS).transpose(0, 2, 1, 3)
    log_i = softcap((i_pre + i_bias).astype(F32)).transpose(0, 2, 1)
    log_f = jax.nn.log_sigmoid(softcap((f_pre + f_bias).astype(F32))).transpose(0, 2, 1)
    h = mlstm_chunkwise(q, k, vh, log_i, log_f).transpose(0, 2, 1, 3)
    h = h * lax.rsqrt(jnp.mean(h * h, -1, keepdims=True) + NORM_EPS)
    h = h * norm_w.astype(F32).reshape(MLSTM_HEADS, MLSTM_V_DIM)
    y = h.reshape(bsz, seq, D_MODEL) * jax.nn.sigmoid(o.astype(F32))
    return y.astype(p.dtype)


def squared_relu_mlp(h, w1, w2):
    return jnp.square(jax.nn.relu(h @ w1)) @ w2


def setup_inputs(seed: int = 0) -> dict:
    key = jax.random.key(seed)
    ks = jax.random.split(key, 32)
    nrm = jax.random.normal
    uni = jax.random.uniform
    E, O, D = N_EVEN, N_ODD, D_MODEL
    return {
        'x': nrm(ks[0], (BATCH, SEQ, D), F32),
        'norm_mix_g': 1.0 + 0.02 * nrm(ks[1], (DEPTH, D), F32),
        'norm_mlp_g': 1.0 + 0.02 * nrm(ks[2], (DEPTH, D), F32),
        'norm_final_g': 1.0 + 0.02 * nrm(ks[3], (D,), F32),
        'ab_w_in': nrm(ks[4], (E, D, AB_COLS), F32) * D ** -0.5,
        'rwkv_mu': uni(ks[5], (E, RWKV_COLS), F32),
        'rwkv_w0': uni(ks[6], (E, RWKV_WIDTH), F32, minval=-6.5, maxval=-1.5),
        'rwkv_w_up': nrm(ks[7], (E, DECAY_LORA, RWKV_WIDTH), F32) * 0.1 * DECAY_LORA ** -0.5,
        'rwkv_a0': 0.1 * nrm(ks[8], (E, RWKV_WIDTH), F32),
        'rwkv_a_up': nrm(ks[9], (E, ICL_LORA, RWKV_WIDTH), F32) * ICL_LORA ** -0.5,
        'rwkv_g_up': nrm(ks[10], (E, GATE_LORA, RWKV_WIDTH), F32) * GATE_LORA ** -0.5,
        'rwkv_k_k': 0.85 + 0.05 * nrm(ks[11], (E, RWKV_WIDTH), F32),
        'rwkv_k_a': 1.0 + 0.05 * nrm(ks[12], (E, RWKV_WIDTH), F32),
        'rwkv_r_k': 0.1 * nrm(ks[13], (E, RWKV_HEADS, RWKV_HEAD_DIM), F32),
        'rwkv_ln_w': 1.0 + 0.02 * nrm(ks[14], (E, RWKV_WIDTH), F32),
        'rwkv_ln_b': 0.02 * nrm(ks[15], (E, RWKV_WIDTH), F32),
        'ab_w_out': nrm(ks[16], (E, D, D), F32) * D ** -0.5,
        'c_w_in': nrm(ks[17], (O, D, C_COLS), F32) * D ** -0.5,
        'c_conv_w': nrm(ks[18], (O, MLSTM_CONV, MLSTM_QK_COLS), F32) * MLSTM_CONV ** -0.5,
        'c_conv_b': 0.02 * nrm(ks[19], (O, MLSTM_QK_COLS), F32),
        'c_i_bias': 0.1 * nrm(ks[20], (O, MLSTM_HEADS), F32),
        'c_f_bias': uni(ks[21], (O, MLSTM_HEADS), F32, minval=3.0, maxval=6.0),
        'c_norm_w': 1.0 + 0.02 * nrm(ks[22], (O, D), F32),
        'c_w_out': nrm(ks[23], (O, D, D), F32) * D ** -0.5,
        'mlp_w1': nrm(ks[24], (DEPTH, D, D_FF), F32) * D ** -0.5,
        'mlp_w2': nrm(ks[25], (DEPTH, D_FF, D), F32) * D_FF ** -0.5,
    }


def reference(x, norm_mix_g, norm_mlp_g, norm_final_g, ab_w_in, rwkv_mu, rwkv_w0,
              rwkv_w_up, rwkv_a0, rwkv_a_up, rwkv_g_up, rwkv_k_k, rwkv_k_a, rwkv_r_k,
              rwkv_ln_w, rwkv_ln_b, ab_w_out, c_w_in, c_conv_w, c_conv_b, c_i_bias,
              c_f_bias, c_norm_w, c_w_out, mlp_w1, mlp_w2):
    seq = x.shape[1]
    pos = jnp.arange(seq, dtype=F32)
    for layer in range(DEPTH):
        h = rms_norm(x, norm_mix_g[layer])
        j = layer // 2
        if layer % 2 == 0:
            p = h @ ab_w_in[j]
            y_a = rwkv7_group(p[..., :RWKV_COLS], rwkv_mu[j], rwkv_w0[j], rwkv_w_up[j],
                              rwkv_a0[j], rwkv_a_up[j], rwkv_g_up[j], rwkv_k_k[j],
                              rwkv_k_a[j], rwkv_r_k[j], rwkv_ln_w[j], rwkv_ln_b[j])
            y_b = retention_group(p[..., RWKV_COLS:], pos)
            y = jnp.concatenate([y_a, y_b], axis=-1) @ ab_w_out[j]
        else:
            p = h @ c_w_in[j]
            y = mlstm_group(p, c_conv_w[j], c_conv_b[j], c_i_bias[j], c_f_bias[j],
                            c_norm_w[j]) @ c_w_out[j]
        x = x + y.astype(x.dtype)
        h = rms_norm(x, norm_mlp_g[layer])
        x = x + squared_relu_mlp(h, mlp_w1[layer], mlp_w2[layer]).astype(x.dtype)
    return rms_norm(x, norm_final_g)
```

```python
import functools
import math

import jax
import jax.numpy as jnp
from jax import lax
from jax.experimental import pallas as pl
from jax.experimental.pallas import tpu as pltpu

F32 = jnp.float32
BF16 = jnp.bfloat16

D_MODEL = 1024
D_FF = 4 * D_MODEL
NORM_EPS = 1e-6

RWKV_WIDTH = 512
RWKV_HEAD = 64
RWKV_LORA = 128
RWKV_GATE_LORA = 128
RWKV_COLS = 3 * RWKV_WIDTH + RWKV_LORA + RWKV_GATE_LORA
RWKV_GN_EPS = 64e-5
RWKV_CHUNK = 64

RET_WIDTH = 512
RET_HEADS = 4
RET_HEAD = 128
RET_COLS = 4 * RET_WIDTH
ROPE_BASE = 10000.0
CHUNK = 128

ML_HEADS = 8
ML_QK = 64
ML_V = 128
ML_CONV = 4
GATE_SOFTCAP = 15.0
GATE_PAD = 128

LANES = 128
MXU_TILE = 256
VMEM_LIMIT = 56 * 1024 * 1024


def _iota(shape, dim):
    return lax.broadcasted_iota(jnp.int32, shape, dim)


def _blk(shape, dim, n):
    return lax.shift_right_logical(_iota(shape, dim), int(math.log2(n)))


def _off(shape, dim, n):
    return _iota(shape, dim) & (n - 1)


def _bdot(a, b):
    return jnp.dot(a.astype(BF16), b.astype(BF16), preferred_element_type=F32)


def _bdot_nt(a, b):
    return lax.dot_general(a.astype(BF16), b.astype(BF16), (((1,), (1,)), ((), ())),
                           preferred_element_type=F32)


def _bdot_tn(a, b):
    return lax.dot_general(a.astype(BF16), b.astype(BF16), (((0,), (0,)), ((), ())),
                           preferred_element_type=F32)


def _split(x, n):
    parts = []
    rem = x
    for _ in range(n):
        p = rem.astype(BF16)
        parts.append(p)
        rem = rem - p.astype(F32)
    return parts


def _xdot_l(m01, x, n=2):
    return sum(jnp.dot(m01, p, preferred_element_type=F32) for p in _split(x, n))


def _xdot_r(x, m01, n=2):
    return sum(jnp.dot(p, m01, preferred_element_type=F32) for p in _split(x, n))


def _sigmoid(x):
    return 1.0 / (1.0 + jnp.exp(-x))


def _rms(x, g):
    return x * lax.rsqrt(jnp.mean(x * x, axis=-1, keepdims=True) + NORM_EPS) * g


def _tile_rows(x, reps):
    return jnp.concatenate([x] * reps, axis=0)


def _shift_rows(x, prev8, k):
    rolled = pltpu.roll(x, k, 0)
    head = jnp.where(_iota((8, 1), 0) < k, pltpu.roll(prev8, k, 0), rolled[0:8])
    return jnp.concatenate([head, rolled[8:]], axis=0)


def _const_spec(shape):
    return pl.BlockSpec(shape, lambda *_: (0,) * len(shape))


def _params(n_grid):
    return pltpu.CompilerParams(dimension_semantics=("arbitrary",) * n_grid,
                                vmem_limit_bytes=VMEM_LIMIT)


def _norm_proj_kernel(x_ref, g_ref, *refs, n_out):
    w_refs, o_refs = refs[:n_out], refs[n_out:]
    h = _rms(x_ref[...], g_ref[...]).astype(BF16)
    for w_ref, o_ref in zip(w_refs, o_refs):
        o_ref[...] = jnp.dot(h, w_ref[...], preferred_element_type=F32).astype(o_ref.dtype)


def _norm_proj(x2d, g, weights, tm):
    n = x2d.shape[0]
    in_specs = [pl.BlockSpec((tm, D_MODEL), lambda i: (i, 0)), _const_spec((1, D_MODEL))]
    in_specs += [_const_spec(w.shape) for w in weights]
    out_specs = [pl.BlockSpec((tm, w.shape[1]), lambda i: (i, 0)) for w in weights]
    out_shape = [jax.ShapeDtypeStruct((n, w.shape[1]), F32) for w in weights]
    return pl.pallas_call(
        functools.partial(_norm_proj_kernel, n_out=len(weights)),
        grid=(n // tm,), in_specs=in_specs, out_specs=out_specs, out_shape=out_shape,
        compiler_params=_params(1), name="norm_proj",
    )(x2d, g.reshape(1, D_MODEL), *weights)


def _mix_mlp_kernel(*refs, n_y, final):
    y_refs, rest = refs[:n_y], refs[n_y:]
    x_ref, g_ref, gf_ref = rest[0], rest[1], rest[2]
    wo_refs, (w1_ref, w2_ref, o_ref) = rest[3:3 + n_y], rest[3 + n_y:]
    x1 = x_ref[...]
    for y_ref, wo_ref in zip(y_refs, wo_refs):
        x1 = x1 + jnp.dot(y_ref[...], wo_ref[...], preferred_element_type=F32)
    h = _rms(x1, g_ref[...]).astype(BF16)
    u = jnp.dot(h, w1_ref[...], preferred_element_type=F32)
    u = jnp.square(jnp.maximum(u, 0.0)).astype(BF16)
    x2 = x1 + jnp.dot(u, w2_ref[...], preferred_element_type=F32)
    if final:
        x2 = _rms(x2, gf_ref[...])
    o_ref[...] = x2


def _mix_mlp(ys, x2d, g, gf, w_outs, w1, w2, tm, final):
    n = x2d.shape[0]
    once = pl.Buffered(1)
    in_specs = [pl.BlockSpec((tm, y.shape[1]), lambda i: (i, 0)) for y in ys]
    in_specs += [pl.BlockSpec((tm, D_MODEL), lambda i: (i, 0)),
                 _const_spec((1, D_MODEL)), _const_spec((1, D_MODEL))]
    in_specs += [pl.BlockSpec(w.shape, lambda i: (0, 0), pipeline_mode=once) for w in w_outs]
    in_specs += [pl.BlockSpec(w1.shape, lambda i: (0, 0), pipeline_mode=once),
                 pl.BlockSpec(w2.shape, lambda i: (0, 0), pipeline_mode=once)]
    return pl.pallas_call(
        functools.partial(_mix_mlp_kernel, n_y=len(ys), final=final),
        grid=(n // tm,), in_specs=in_specs,
        out_specs=pl.BlockSpec((tm, D_MODEL), lambda i: (i, 0)),
        out_shape=jax.ShapeDtypeStruct((n, D_MODEL), F32),
        compiler_params=_params(1), name="mix_mlp",
    )(*ys, x2d, g.reshape(1, D_MODEL), gf.reshape(1, D_MODEL), *w_outs, w1, w2)


def _rwkv_kernel(p_ref, mu_ref, wa_ref, gup_ref, vec_ref, o_ref, prev_scr, st_scr, y_scr, *, tc):
    T = RWKV_CHUNK
    G = MXU_TILE
    n_groups = RWKV_WIDTH // G
    n_chunks = tc // T

    @pl.when(pl.program_id(1) == 0)
    def _():
        prev_scr[...] = jnp.zeros_like(prev_scr)
        st_scr[...] = jnp.zeros_like(st_scr)

    p_raw = p_ref[0].astype(F32)
    shifted = _shift_rows(p_raw, prev_scr[...], 1)
    prev_scr[...] = p_raw[tc - 8:tc]
    p = p_raw + (shifted - p_raw) * mu_ref[...]

    w0, a0, k_k, k_a = (vec_ref[i:i + 1] for i in range(4))
    r_k, ln_w, ln_b = (vec_ref[i:i + 1] for i in range(4, 7))
    W = RWKV_WIDTH
    r, k, v = p[:, 0:W], p[:, W:2 * W], p[:, 2 * W:3 * W]
    lora = p[:, 3 * W:3 * W + RWKV_LORA]
    g_lo = p[:, 3 * W + RWKV_LORA:]

    lora = jnp.where(_iota((1, RWKV_LORA), 1) < RWKV_LORA // 2, jnp.tanh(lora), lora)
    za = _bdot(lora, wa_ref[...])
    ld = (-math.exp(-0.5)) * _sigmoid(w0 + za[:, :W])
    a = _sigmoid(a0 + za[:, W:])
    g = _bdot(_sigmoid(g_lo), gup_ref[...])

    head_ones = (_blk((W, W), 0, RWKV_HEAD) == _blk((W, W), 1, RWKV_HEAD)).astype(BF16)
    kk = k * k_k
    kk = kk * lax.rsqrt(jnp.maximum(_xdot_r(kk * kk, head_ones), 1e-24))
    k = k * (1.0 + (a - 1.0) * k_a)
    a_s = -kk
    b_s = kk * a
    bonus = _xdot_r(r * k * r_k, head_ones) * v

    ri, ci = _iota((tc, tc), 0), _iota((tc, tc), 1)
    same = _blk((tc, tc), 0, T) == _blk((tc, tc), 1, T)
    cum = _xdot_l(jnp.concatenate([(same & (ci <= ri)).astype(BF16), same.astype(BF16)], axis=0), ld)
    L, LT = cum[:tc], cum[tc:]
    e_neg = jnp.exp(-L)
    e_end = jnp.exp(LT - L)
    rt = r * jnp.exp(L)
    at = a_s * jnp.exp(L - ld)
    bt, kt = b_s * e_neg, k * e_neg
    bh, kh = b_s * e_end, k * e_end
    w_end = jnp.exp(LT)

    row, col = _iota((T, G), 0), _off((T, G), 1, T)
    strict, incl = col < row, col <= row
    eye_cat = (col == row).astype(F32)
    bd_mask = _blk((G, G), 0, T) == _blk((G, G), 1, T)
    eye_g = _iota((G, G), 0) == _iota((G, G), 1)

    def bd(x):
        return jnp.where(bd_mask, _tile_rows(x.astype(BF16), G // T), jnp.zeros((), BF16))

    for gi in range(n_groups):
        ls = slice(gi * G, (gi + 1) * G)
        pre = []
        for c in range(n_chunks):
            rs = slice(c * T, (c + 1) * T)
            sl = (rs, ls)
            lhs = jnp.concatenate([at[sl], rt[sl]], axis=0)
            a1 = _bdot_nt(lhs, bd(bt[sl]))
            a2 = _bdot_nt(lhs, bd(kt[sl]))
            a_ab = jnp.where(strict, a1[:T], 0.0)
            a_rb = jnp.where(incl, a1[T:], 0.0)
            a_ak = jnp.where(strict, a2[:T], 0.0)
            a_rk = jnp.where(incl, a2[T:], 0.0)
            acc = eye_cat + a_ab
            pw = _bdot(a_ab, bd(a_ab))
            for _ in range(int(math.log2(T)) - 2):
                both = _bdot(jnp.concatenate([pw, acc], axis=0), bd(pw))
                pw, acc = both[:T], acc + both[T:]
            tinv = acc + _bdot(acc, bd(pw))
            av = _bdot(a_ak, bd(v[sl]))
            pu = _bdot(tinv, jnp.concatenate([bd(at[sl]), bd(av)], axis=1))
            p_c, u0 = pu[:, :G], pu[:, G:]
            m_bd = jnp.where(bd_mask, _bdot_tn(bh[sl], p_c), 0.0) + jnp.where(eye_g, w_end[rs, ls][0:1], 0.0)
            n_bd = jnp.where(bd_mask, _bdot_tn(jnp.concatenate([bh[sl], kh[sl]], axis=0),
                                               jnp.concatenate([u0, v[sl]], axis=0)), 0.0)
            q_c = rt[sl] + _bdot(a_rb, bd(p_c))
            y0 = _bdot(jnp.concatenate([a_rb, a_rk], axis=1),
                       jnp.concatenate([bd(u0), bd(v[sl])], axis=0))
            pre.append((m_bd, n_bd, q_c, y0))
        st = st_scr[gi]
        for c, (m_bd, n_bd, q_c, y0) in enumerate(pre):
            y_scr[c * T:(c + 1) * T, ls] = _bdot(q_c, st) + y0
            st = _bdot(m_bd, st) + n_bd
        st_scr[gi] = st

    y = y_scr[...]
    mean = _xdot_r(y, head_ones) * (1.0 / RWKV_HEAD)
    yc = y - mean
    var = _xdot_r(yc * yc, head_ones) * (1.0 / RWKV_HEAD)
    y = yc * lax.rsqrt(var + RWKV_GN_EPS) * ln_w + ln_b
    o_ref[0] = ((y + bonus) * g).astype(o_ref.dtype)


def _rwkv(p, mu, wa, gup, vec, tc):
    b, t, _ = p.shape
    return pl.pallas_call(
        functools.partial(_rwkv_kernel, tc=tc),
        grid=(b, t // tc),
        in_specs=[pl.BlockSpec((1, tc, RWKV_COLS), lambda i, j: (i, j, 0)),
                  _const_spec(mu.shape), _const_spec(wa.shape), _const_spec(gup.shape),
                  _const_spec(vec.shape)],
        out_specs=pl.BlockSpec((1, tc, RWKV_WIDTH), lambda i, j: (i, j, 0)),
        out_shape=jax.ShapeDtypeStruct((b, t, RWKV_WIDTH), BF16),
        scratch_shapes=[pltpu.VMEM((8, RWKV_COLS), F32),
                        pltpu.VMEM((RWKV_WIDTH // MXU_TILE, MXU_TILE, MXU_TILE), F32),
                        pltpu.VMEM((tc, RWKV_WIDTH), F32)],
        compiler_params=_params(2), name="rwkv7",
    )(p, mu, wa, gup, vec)


def _retention_kernel(p_ref, cos_ref, sin_ref, o_ref, st_scr, o_scr, *, tc):
    L = CHUNK
    G = MXU_TILE
    n_groups = RET_WIDTH // G
    hpg = G // RET_HEAD

    @pl.when(pl.program_id(1) == 0)
    def _():
        st_scr[...] = jnp.zeros_like(st_scr)

    W = RET_WIDTH
    p = p_ref[0].astype(F32)
    cos, sin = cos_ref[...], sin_ref[...]

    def rope(x):
        return jnp.concatenate(
            [x[:, h * RET_HEAD:(h + 1) * RET_HEAD] * cos
             + pltpu.roll(x[:, h * RET_HEAD:(h + 1) * RET_HEAD], RET_HEAD // 2, 1) * sin
             for h in range(RET_HEADS)], axis=1)

    q = rope(p[:, 0:W])
    k = rope(p[:, W:2 * W]) * (RET_HEAD ** -0.5)
    v = p[:, 2 * W:3 * W]
    gate = p[:, 3 * W:]

    def lane_gamma(gi):
        lg = [math.log1p(-2.0 ** (-5.0 - (gi * hpg + h))) for h in range(hpg)]
        out = jnp.full((1, G), lg[-1], F32)
        for h in range(hpg - 1, -1, -1):
            out = jnp.where(_iota((1, G), 1) < (h + 1) * RET_HEAD, lg[h], out)
        return out

    rel = (_iota((L, G), 0) - _off((L, G), 1, L)).astype(F32)
    pos = _iota((L, G), 0).astype(F32)
    bd_mask = _blk((G, G), 0, RET_HEAD) == _blk((G, G), 1, RET_HEAD)

    def bd(x):
        return jnp.where(bd_mask, _tile_rows(x.astype(BF16), hpg), jnp.zeros((), BF16))

    for gi in range(n_groups):
        ls = slice(gi * G, (gi + 1) * G)
        lg = lane_gamma(gi)
        decay = jnp.where(rel >= 0, jnp.exp(lg * jnp.maximum(rel, 0.0)), 0.0)
        q_dec = jnp.exp(lg * (pos + 1.0))
        k_dec = jnp.exp(lg * (L - 1.0 - pos))
        c_dec = jnp.exp(lg * float(L))
        c_rows = jnp.concatenate([jnp.broadcast_to(c_dec[:, h * RET_HEAD:h * RET_HEAD + 1], (RET_HEAD, 1))
                                  for h in range(hpg)], axis=0)
        st = st_scr[gi]
        for c in range(tc // L):
            sl = (slice(c * L, (c + 1) * L), ls)
            s = _bdot_nt(q[sl], bd(k[sl])) * decay
            o = _bdot(s, bd(v[sl])) + _bdot(q[sl], st) * q_dec
            o_scr[sl] = o
            st = st * c_rows + jnp.where(bd_mask, _bdot_tn(k[sl] * k_dec, v[sl]), 0.0)
        st_scr[gi] = st

    o = o_scr[...]
    outs = []
    for h in range(RET_HEADS):
        oh = o[:, h * RET_HEAD:(h + 1) * RET_HEAD]
        outs.append(oh * lax.rsqrt(jnp.mean(oh * oh, axis=-1, keepdims=True) + NORM_EPS))
    o = jnp.concatenate(outs, axis=1)
    o_ref[0] = (o * (gate * _sigmoid(gate))).astype(o_ref.dtype)


def _retention(p, cos, sin, tc):
    b, t, _ = p.shape
    return pl.pallas_call(
        functools.partial(_retention_kernel, tc=tc),
        grid=(b, t // tc),
        in_specs=[pl.BlockSpec((1, tc, RET_COLS), lambda i, j: (i, j, 0)),
                  pl.BlockSpec((tc, RET_HEAD), lambda i, j: (j, 0)),
                  pl.BlockSpec((tc, RET_HEAD), lambda i, j: (j, 0))],
        out_specs=pl.BlockSpec((1, tc, RET_WIDTH), lambda i, j: (i, j, 0)),
        out_shape=jax.ShapeDtypeStruct((b, t, RET_WIDTH), BF16),
        scratch_shapes=[pltpu.VMEM((RET_WIDTH // MXU_TILE, MXU_TILE, MXU_TILE), F32),
                        pltpu.VMEM((tc, RET_WIDTH), F32)],
        compiler_params=_params(2), name="retention",
    )(p, cos, sin)


def _mlstm_kernel(qk_ref, v_ref, og_ref, gt_ref, cw_ref, vec_ref, gb_ref, o_ref,
                  prev_scr, c_scr, n_scr, m_scr, h_scr, *, tc):
    L = CHUNK
    G = MXU_TILE
    hpg = G // ML_V
    GK = hpg * ML_QK
    n_groups = ML_HEADS // hpg
    QW = ML_HEADS * ML_QK

    @pl.when(pl.program_id(1) == 0)
    def _():
        prev_scr[...] = jnp.zeros_like(prev_scr)
        c_scr[...] = jnp.zeros_like(c_scr)
        n_scr[...] = jnp.zeros_like(n_scr)
        m_scr[...] = jnp.zeros_like(m_scr)

    conv_b, norm_w = vec_ref[0:1], vec_ref[1:2]
    x = qk_ref[0].astype(F32)
    prev8 = prev_scr[...]
    prev_scr[...] = x[tc - 8:tc]
    acc = x * cw_ref[ML_CONV - 1:ML_CONV] + conv_b
    for j in range(ML_CONV - 1):
        acc = acc + _shift_rows(x, prev8, ML_CONV - 1 - j) * cw_ref[j:j + 1]
    qk = acc * _sigmoid(acc)
    q, k = qk[:, :QW], qk[:, QW:] * (ML_QK ** -0.5)
    v = v_ref[0].astype(F32)

    pre = gt_ref[0].astype(F32) + gb_ref[...]
    cap = GATE_SOFTCAP * jnp.tanh(pre * (1.0 / GATE_SOFTCAP))
    log_sig = jnp.minimum(cap, 0.0) - jnp.log(1.0 + jnp.exp(-jnp.abs(cap)))
    gates = jnp.where(_iota((1, GATE_PAD), 1) < ML_HEADS, cap, log_sig)

    tri = (_iota((L, L), 1) <= _iota((L, L), 0)).astype(BF16)
    tri_t = (_iota((L, L), 0) <= _iota((L, L), 1)).astype(BF16)
    causal = _iota((L, L), 1) <= _iota((L, L), 0)
    k_mask = _blk((G, GK), 0, L) == _blk((G, GK), 1, ML_QK)
    v_mask = _blk((G, G), 0, L) == _blk((G, G), 1, ML_V)
    c_mask = _blk((GK, G), 0, ML_QK) == _blk((GK, G), 1, ML_V)
    lane_k = _iota((1, GK), 1)

    for c in range(tc // L):
        rs = slice(c * L, (c + 1) * L)
        g_c = gates[rs]
        g_t = g_c.T
        b_col = _xdot_l(tri, g_c, 3)
        b_row = _xdot_r(g_t, tri_t, 3)
        for gi in range(n_groups):
            kl = slice(gi * GK, (gi + 1) * GK)
            vl = slice(gi * G, (gi + 1) * G)
            q_g, k_g, v_g = q[rs, kl], k[rs, kl], v[rs, vl]
            c_st, n_st = c_scr[gi], n_scr[gi]
            dms, inters, m_ts, kw_f, cs_l, m_news = [], [], [], [], [], []
            for h in range(hpg):
                hh = gi * hpg + h
                fi = ML_HEADS + hh
                m_old = m_scr[hh:hh + 1, 0:1]
                bc = b_col[:, fi:fi + 1]
                br = b_row[fi:fi + 1, :]
                li_r = g_t[hh:hh + 1, :]
                li_c = g_c[:, hh:hh + 1]
                log_d = jnp.where(causal, bc - br + li_r, -jnp.inf)
                log_inter = bc + m_old
                m_t = jnp.maximum(jnp.max(log_d, axis=-1, keepdims=True), log_inter)
                dms.append(jnp.exp(log_d - m_t))
                inters.append(jnp.exp(log_inter - m_t))
                m_ts.append(m_t)
                b_end = bc[L - 1:L]
                m_new = jnp.maximum(b_end + m_old,
                                    jnp.max(b_end - br + li_r, axis=-1, keepdims=True))
                kw_f.append(jnp.exp(b_end - bc + li_c - m_new))
                cs_l.append(jnp.exp(b_end + m_old - m_new))
                m_news.append(m_new)
                m_scr[hh:hh + 1, :] = jnp.broadcast_to(m_new, (1, LANES))

            k_bd = jnp.where(k_mask, _tile_rows(k_g.astype(BF16), hpg), jnp.zeros((), BF16))
            v_bd = jnp.where(v_mask, _tile_rows(v_g.astype(BF16), hpg), jnp.zeros((), BF16))
            s = _bdot_nt(q_g, k_bd) * jnp.concatenate(dms, axis=1)
            num = _bdot(s, v_bd)
            qc = _bdot(q_g, c_st)
            qn = q_g * n_st
            outs = []
            for h in range(hpg):
                seg = _blk((1, GK), 1, ML_QK) == h
                den = (jnp.sum(s[:, h * L:(h + 1) * L], axis=-1, keepdims=True)
                       + inters[h] * jnp.sum(jnp.where(seg, qn, 0.0), axis=-1, keepdims=True))
                den = jnp.maximum(jnp.abs(den), jnp.exp(-m_ts[h]))
                hs = slice(h * ML_V, (h + 1) * ML_V)
                outs.append((num[:, hs] + inters[h] * qc[:, hs]) / den)
            h_scr[rs, vl] = jnp.concatenate(outs, axis=1)

            kw_fac = kw_f[hpg - 1]
            cs_lane = cs_l[hpg - 1]
            for h in range(hpg - 2, -1, -1):
                kw_fac = jnp.where(lane_k < (h + 1) * ML_QK, kw_f[h], kw_fac)
                cs_lane = jnp.where(lane_k < (h + 1) * ML_QK, cs_l[h], cs_lane)
            kw = k_g * kw_fac
            cs_rows = jnp.concatenate([jnp.broadcast_to(cs_l[h], (ML_QK, 1)) for h in range(hpg)], axis=0)
            c_scr[gi] = cs_rows * c_st + jnp.where(c_mask, _bdot_tn(kw, v_g), 0.0)
            n_scr[gi] = cs_lane * n_st + jnp.sum(kw, axis=0, keepdims=True)

    hfull = h_scr[...]
    outs = []
    for h in range(ML_HEADS):
        hh = hfull[:, h * ML_V:(h + 1) * ML_V]
        outs.append(hh * lax.rsqrt(jnp.mean(hh * hh, axis=-1, keepdims=True) + NORM_EPS))
    hn = jnp.concatenate(outs, axis=1) * norm_w
    o_ref[0] = (hn * _sigmoid(og_ref[0].astype(F32))).astype(o_ref.dtype)


def _mlstm(qk, v, og, gt, conv_w, vec, gbias, tc):
    b, t, _ = qk.shape
    hpg = MXU_TILE // ML_V
    n_groups = ML_HEADS // hpg
    tile = lambda w: pl.BlockSpec((1, tc, w), lambda i, j: (i, j, 0))
    return pl.pallas_call(
        functools.partial(_mlstm_kernel, tc=tc),
        grid=(b, t // tc),
        in_specs=[tile(D_MODEL), tile(D_MODEL), tile(D_MODEL), tile(GATE_PAD),
                  _const_spec(conv_w.shape), _const_spec(vec.shape), _const_spec(gbias.shape)],
        out_specs=tile(D_MODEL),
        out_shape=jax.ShapeDtypeStruct((b, t, D_MODEL), BF16),
        scratch_shapes=[pltpu.VMEM((8, D_MODEL), F32),
                        pltpu.VMEM((n_groups, hpg * ML_QK, MXU_TILE), F32),
                        pltpu.VMEM((n_groups, 1, hpg * ML_QK), F32),
                        pltpu.VMEM((ML_HEADS, LANES), F32),
                        pltpu.VMEM((tc, D_MODEL), F32)],
        compiler_params=_params(2), name="mlstm",
    )(qk, v, og, gt, conv_w, vec, gbias)


def kernel(x, norm_mix_g, norm_mlp_g, norm_final_g, ab_w_in, rwkv_mu, rwkv_w0, rwkv_w_up, rwkv_a0,
           rwkv_a_up, rwkv_g_up, rwkv_k_k, rwkv_k_a, rwkv_r_k, rwkv_ln_w, rwkv_ln_b, ab_w_out,
           c_w_in, c_conv_w, c_conv_b, c_i_bias, c_f_bias, c_norm_w, c_w_out, mlp_w1, mlp_w2):
    bsz, seq, d = x.shape
    n = bsz * seq
    tm = 512
    tc = 256
    bf = lambda w: w.astype(BF16)
    x2d = x.reshape(n, d)

    w_in = ab_w_in[0]
    p_a, p_b = _norm_proj(x2d, norm_mix_g[0], [bf(w_in[:, :RWKV_COLS]), bf(w_in[:, RWKV_COLS:])], tm)
    half = RWKV_LORA // 2
    zeros = jnp.zeros((half, RWKV_WIDTH), F32)
    wa = bf(jnp.concatenate([jnp.concatenate([rwkv_w_up[0], zeros], axis=1),
                             jnp.concatenate([zeros, rwkv_a_up[0]], axis=1)], axis=0))
    vec = jnp.stack([rwkv_w0[0], rwkv_a0[0], rwkv_k_k[0], rwkv_k_a[0], rwkv_r_k[0].reshape(-1),
                     rwkv_ln_w[0], rwkv_ln_b[0], jnp.zeros((RWKV_WIDTH,), F32)])
    y_a = _rwkv(p_a.reshape(bsz, seq, RWKV_COLS), rwkv_mu[0].reshape(1, RWKV_COLS), wa,
                bf(rwkv_g_up[0]), vec, tc)

    pos = jnp.arange(seq, dtype=F32)
    inv = ROPE_BASE ** (-jnp.arange(0, RET_HEAD, 2, dtype=F32) / RET_HEAD)
    ang = pos[:, None] * inv[None, :]
    cos, sin = jnp.cos(ang), jnp.sin(ang)
    y_b = _retention(p_b.reshape(bsz, seq, RET_COLS), jnp.concatenate([cos, cos], axis=1),
                     jnp.concatenate([-sin, sin], axis=1), tc)

    w_out = bf(ab_w_out[0])
    x2d = _mix_mlp([y_a.reshape(n, RWKV_WIDTH), y_b.reshape(n, RET_WIDTH)], x2d, norm_mlp_g[0],
                   norm_final_g, [w_out[:RWKV_WIDTH], w_out[RWKV_WIDTH:]], bf(mlp_w1[0]), bf(mlp_w2[0]),
                   tm, final=False)

    w_in = c_w_in[0]
    qk_cols = 2 * ML_HEADS * ML_QK
    w_g = jnp.pad(w_in[:, qk_cols + 2 * d:], ((0, 0), (0, GATE_PAD - 2 * ML_HEADS)))
    qk, v, og, gt = _norm_proj(x2d, norm_mix_g[1],
                               [bf(w_in[:, :qk_cols]), bf(w_in[:, qk_cols:qk_cols + d]),
                                bf(w_in[:, qk_cols + d:qk_cols + 2 * d]), bf(w_g)], tm)
    gbias = jnp.pad(jnp.concatenate([c_i_bias[0], c_f_bias[0]]), (0, GATE_PAD - 2 * ML_HEADS)).reshape(1, GATE_PAD)
    vec = jnp.stack([c_conv_b[0], c_norm_w[0]] + [jnp.zeros((d,), F32)] * 6)
    shp = lambda a: a.reshape(bsz, seq, a.shape[-1])
    y_c = _mlstm(shp(qk), shp(v), shp(og), shp(gt), c_conv_w[0], vec, gbias, tc)

    x2d = _mix_mlp([y_c.reshape(n, d)], x2d, norm_mlp_g[1], norm_final_g, [bf(c_w_out[0])],
                   bf(mlp_w1[1]), bf(mlp_w2[1]), tm, final=True)
    return x2d.reshape(bsz, seq, d)
```

```python
import functools
import math

import jax
import jax.numpy as jnp
from jax import lax
from jax.experimental import pallas as pl
from jax.experimental.pallas import tpu as pltpu

F32 = jnp.float32
BF16 = jnp.bfloat16

D_MODEL = 1024
D_FF = 4 * D_MODEL
NORM_EPS = 1e-6

RWKV_WIDTH = 512
RWKV_HEAD = 64
RWKV_LORA = 128
RWKV_GATE_LORA = 128
RWKV_COLS = 3 * RWKV_WIDTH + RWKV_LORA + RWKV_GATE_LORA
RWKV_GN_EPS = 64e-5
RWKV_CHUNK = 64

RET_WIDTH = 512
RET_HEADS = 4
RET_HEAD = 128
RET_COLS = 4 * RET_WIDTH
ROPE_BASE = 10000.0
CHUNK = 128

ML_HEADS = 8
ML_QK = 64
ML_V = 128
ML_CONV = 4
GATE_SOFTCAP = 15.0
LOG2E = math.log2(math.e)
GATE_PAD = 128

LANES = 128
MXU_TILE = 256
VMEM_LIMIT = 56 * 1024 * 1024


def _iota(shape, dim):
    return lax.broadcasted_iota(jnp.int32, shape, dim)


def _blk(shape, dim, n):
    return lax.shift_right_logical(_iota(shape, dim), int(math.log2(n)))


def _off(shape, dim, n):
    return _iota(shape, dim) & (n - 1)


def _bdot(a, b):
    return jnp.dot(a.astype(BF16), b.astype(BF16), preferred_element_type=F32)


def _bdot_nt(a, b):
    return lax.dot_general(a.astype(BF16), b.astype(BF16), (((1,), (1,)), ((), ())),
                           preferred_element_type=F32)


def _bdot_tn(a, b):
    return lax.dot_general(a.astype(BF16), b.astype(BF16), (((0,), (0,)), ((), ())),
                           preferred_element_type=F32)


def _split(x, n):
    parts = []
    rem = x
    for _ in range(n):
        p = rem.astype(BF16)
        parts.append(p)
        rem = rem - p.astype(F32)
    return parts


def _xdot_l(m01, x, n=2):
    return sum(jnp.dot(m01, p, preferred_element_type=F32) for p in _split(x, n))


def _xdot_r(x, m01, n=2):
    return sum(jnp.dot(p, m01, preferred_element_type=F32) for p in _split(x, n))


def _sigmoid(x):
    return 1.0 / (1.0 + jnp.exp(-x))


def _rms(x, g):
    return x * lax.rsqrt(jnp.mean(x * x, axis=-1, keepdims=True) + NORM_EPS) * g


def _tile_rows(x, reps):
    return jnp.concatenate([x] * reps, axis=0)


def _shift_rows(x, prev8, k):
    rolled = pltpu.roll(x, k, 0)
    head = jnp.where(_iota((8, 1), 0) < k, pltpu.roll(prev8, k, 0), rolled[0:8])
    return jnp.concatenate([head, rolled[8:]], axis=0)


def _const_spec(shape):
    return pl.BlockSpec(shape, lambda *_: (0,) * len(shape))


def _params(n_grid):
    return pltpu.CompilerParams(dimension_semantics=("arbitrary",) * n_grid,
                                vmem_limit_bytes=VMEM_LIMIT)


def _norm_proj_kernel(x_ref, g_ref, *refs, n_out):
    w_refs, o_refs = refs[:n_out], refs[n_out:]
    h = _rms(x_ref[...], g_ref[...]).astype(BF16)
    for w_ref, o_ref in zip(w_refs, o_refs):
        o_ref[...] = jnp.dot(h, w_ref[...], preferred_element_type=F32).astype(o_ref.dtype)


def _norm_proj(x2d, g, weights, tm):
    n = x2d.shape[0]
    in_specs = [pl.BlockSpec((tm, D_MODEL), lambda i: (i, 0)), _const_spec((1, D_MODEL))]
    in_specs += [_const_spec(w.shape) for w in weights]
    out_specs = [pl.BlockSpec((tm, w.shape[1]), lambda i: (i, 0)) for w in weights]
    out_shape = [jax.ShapeDtypeStruct((n, w.shape[1]), F32) for w in weights]
    return pl.pallas_call(
        functools.partial(_norm_proj_kernel, n_out=len(weights)),
        grid=(n // tm,), in_specs=in_specs, out_specs=out_specs, out_shape=out_shape,
        compiler_params=_params(1), name="norm_proj",
    )(x2d, g.reshape(1, D_MODEL), *weights)


def _mix_mlp_kernel(*refs, n_y, final):
    y_refs, rest = refs[:n_y], refs[n_y:]
    x_ref, g_ref, gf_ref = rest[0], rest[1], rest[2]
    wo_refs, (w1_ref, w2_ref, o_ref) = rest[3:3 + n_y], rest[3 + n_y:]
    x1 = x_ref[...]
    for y_ref, wo_ref in zip(y_refs, wo_refs):
        x1 = x1 + jnp.dot(y_ref[...], wo_ref[...], preferred_element_type=F32)
    h = _rms(x1, g_ref[...]).astype(BF16)
    u = jnp.dot(h, w1_ref[...], preferred_element_type=F32)
    u = jnp.square(jnp.maximum(u, 0.0)).astype(BF16)
    x2 = x1 + jnp.dot(u, w2_ref[...], preferred_element_type=F32)
    if final:
        x2 = _rms(x2, gf_ref[...])
    o_ref[...] = x2


def _mix_mlp(ys, x2d, g, gf, w_outs, w1, w2, tm, final):
    n = x2d.shape[0]
    once = pl.Buffered(1)
    in_specs = [pl.BlockSpec((tm, y.shape[1]), lambda i: (i, 0)) for y in ys]
    in_specs += [pl.BlockSpec((tm, D_MODEL), lambda i: (i, 0)),
                 _const_spec((1, D_MODEL)), _const_spec((1, D_MODEL))]
    in_specs += [pl.BlockSpec(w.shape, lambda i: (0, 0), pipeline_mode=once) for w in w_outs]
    in_specs += [pl.BlockSpec(w1.shape, lambda i: (0, 0), pipeline_mode=once),
                 pl.BlockSpec(w2.shape, lambda i: (0, 0), pipeline_mode=once)]
    return pl.pallas_call(
        functools.partial(_mix_mlp_kernel, n_y=len(ys), final=final),
        grid=(n // tm,), in_specs=in_specs,
        out_specs=pl.BlockSpec((tm, D_MODEL), lambda i: (i, 0)),
        out_shape=jax.ShapeDtypeStruct((n, D_MODEL), F32),
        compiler_params=_params(1), name="mix_mlp",
    )(*ys, x2d, g.reshape(1, D_MODEL), gf.reshape(1, D_MODEL), *w_outs, w1, w2)


def _rwkv_kernel(p_ref, mu_ref, wa_ref, gup_ref, vec_ref, o_ref, prev_scr, st_scr, y_scr, *, tc):
    T = RWKV_CHUNK
    G = MXU_TILE
    n_groups = RWKV_WIDTH // G
    n_chunks = tc // T

    @pl.when(pl.program_id(1) == 0)
    def _():
        prev_scr[...] = jnp.zeros_like(prev_scr)
        st_scr[...] = jnp.zeros_like(st_scr)

    p_raw = p_ref[0].astype(F32)
    shifted = _shift_rows(p_raw, prev_scr[...], 1)
    prev_scr[...] = p_raw[tc - 8:tc]
    p = p_raw + (shifted - p_raw) * mu_ref[...]

    w0, a0, k_k, k_a = (vec_ref[i:i + 1] for i in range(4))
    r_k, ln_w, ln_b = (vec_ref[i:i + 1] for i in range(4, 7))
    W = RWKV_WIDTH
    r, k, v = p[:, 0:W], p[:, W:2 * W], p[:, 2 * W:3 * W]
    lora = p[:, 3 * W:3 * W + RWKV_LORA]
    g_lo = p[:, 3 * W + RWKV_LORA:]

    lora = jnp.where(_iota((1, RWKV_LORA), 1) < RWKV_LORA // 2, jnp.tanh(lora), lora)
    za = _bdot(lora, wa_ref[...])
    ld = (-math.exp(-0.5) * LOG2E) * _sigmoid(w0 + za[:, :W])
    a = _sigmoid(a0 + za[:, W:])
    g = _bdot(_sigmoid(g_lo), gup_ref[...])

    ones_g = (_blk((G, G), 0, RWKV_HEAD) == _blk((G, G), 1, RWKV_HEAD)).astype(BF16)

    def head_sum(x):
        return jnp.concatenate([_xdot_r(x[:, i * G:(i + 1) * G], ones_g) for i in range(n_groups)], axis=1)

    kk = k * k_k
    kk = kk * lax.rsqrt(jnp.maximum(head_sum(kk * kk), 1e-24))
    k = k * (1.0 + (a - 1.0) * k_a)
    a_s = -kk
    b_s = kk * a
    bonus = head_sum(r * k * r_k) * v

    ri, ci = _iota((tc, tc), 0), _iota((tc, tc), 1)
    same = _blk((tc, tc), 0, T) == _blk((tc, tc), 1, T)
    cum = _xdot_l(jnp.concatenate([(same & (ci <= ri)).astype(BF16), same.astype(BF16)], axis=0), ld)
    L, LT = cum[:tc], cum[tc:]
    e_neg = jnp.exp2(-L)
    e_end = jnp.exp2(LT - L)
    rt = r * jnp.exp2(L)
    at = a_s * jnp.exp2(L - ld)
    bt, kt = b_s * e_neg, k * e_neg
    bh, kh = b_s * e_end, k * e_end
    w_end = jnp.exp2(LT)

    row, col = _iota((T, G), 0), _off((T, G), 1, T)
    strict, incl = col < row, col <= row
    eye_cat = (col == row).astype(F32)
    bd_mask = _blk((G, G), 0, T) == _blk((G, G), 1, T)
    eye_g = _iota((G, G), 0) == _iota((G, G), 1)

    def bd(x):
        return jnp.where(bd_mask, _tile_rows(x.astype(BF16), G // T), jnp.zeros((), BF16))

    units = [(gi, c) for c in range(n_chunks) for gi in range(n_groups)]
    sl = {u: (slice(u[1] * T, (u[1] + 1) * T), slice(u[0] * G, (u[0] + 1) * G)) for u in units}
    lhs = {u: jnp.concatenate([at[sl[u]], rt[sl[u]]], axis=0) for u in units}
    a1 = {u: _bdot_nt(lhs[u], bd(bt[sl[u]])) for u in units}
    a2 = {u: _bdot_nt(lhs[u], bd(kt[sl[u]])) for u in units}
    a_ab = {u: jnp.where(strict, a1[u][:T], 0.0) for u in units}
    a_rb = {u: jnp.where(incl, a1[u][T:], 0.0) for u in units}
    a_ak = {u: jnp.where(strict, a2[u][:T], 0.0) for u in units}
    a_rk = {u: jnp.where(incl, a2[u][T:], 0.0) for u in units}
    acc = {u: eye_cat + a_ab[u] for u in units}
    pw = {u: _bdot(a_ab[u], bd(a_ab[u])) for u in units}
    av = {u: _bdot(a_ak[u], bd(v[sl[u]])) for u in units}
    for _ in range(int(math.log2(T)) - 2):
        both = {u: _bdot(jnp.concatenate([pw[u], acc[u]], axis=0), bd(pw[u])) for u in units}
        pw = {u: both[u][:T] for u in units}
        acc = {u: acc[u] + both[u][T:] for u in units}
    tinv = {u: acc[u] + _bdot(acc[u], bd(pw[u])) for u in units}
    pu = {u: _bdot(tinv[u], jnp.concatenate([bd(at[sl[u]]), bd(av[u])], axis=1)) for u in units}
    p_c = {u: pu[u][:, :G] for u in units}
    u0 = {u: pu[u][:, G:] for u in units}
    m_bd = {u: jnp.where(bd_mask, _bdot_tn(bh[sl[u]], p_c[u]), 0.0)
            + jnp.where(eye_g, w_end[u[1] * T:u[1] * T + 1, sl[u][1]], 0.0) for u in units}
    n_bd = {u: jnp.where(bd_mask, _bdot_tn(jnp.concatenate([bh[sl[u]], kh[sl[u]]], axis=0),
                                           jnp.concatenate([u0[u], v[sl[u]]], axis=0)), 0.0)
            for u in units}
    q_c = {u: rt[sl[u]] + _bdot(a_rb[u], bd(p_c[u])) for u in units}
    y0 = {u: _bdot(jnp.concatenate([a_rb[u], a_rk[u]], axis=1),
                   jnp.concatenate([bd(u0[u]), bd(v[sl[u]])], axis=0)) for u in units}
    st = [st_scr[gi] for gi in range(n_groups)]
    for u in units:
        gi = u[0]
        y_scr[sl[u]] = _bdot(q_c[u], st[gi]) + y0[u]
        st[gi] = _bdot(m_bd[u], st[gi]) + n_bd[u]
    for gi in range(n_groups):
        st_scr[gi] = st[gi]

    y = y_scr[...]
    mean = head_sum(y) * (1.0 / RWKV_HEAD)
    yc = y - mean
    var = head_sum(yc * yc) * (1.0 / RWKV_HEAD)
    y = yc * lax.rsqrt(var + RWKV_GN_EPS) * ln_w + ln_b
    o_ref[0] = ((y + bonus) * g).astype(o_ref.dtype)


def _rwkv(p, mu, wa, gup, vec, tc):
    b, t, _ = p.shape
    return pl.pallas_call(
        functools.partial(_rwkv_kernel, tc=tc),
        grid=(b, t // tc),
        in_specs=[pl.BlockSpec((1, tc, RWKV_COLS), lambda i, j: (i, j, 0)),
                  _const_spec(mu.shape), _const_spec(wa.shape), _const_spec(gup.shape),
                  _const_spec(vec.shape)],
        out_specs=pl.BlockSpec((1, tc, RWKV_WIDTH), lambda i, j: (i, j, 0)),
        out_shape=jax.ShapeDtypeStruct((b, t, RWKV_WIDTH), BF16),
        scratch_shapes=[pltpu.VMEM((8, RWKV_COLS), F32),
                        pltpu.VMEM((RWKV_WIDTH // MXU_TILE, MXU_TILE, MXU_TILE), F32),
                        pltpu.VMEM((tc, RWKV_WIDTH), F32)],
        compiler_params=_params(2), name="rwkv7",
    )(p, mu, wa, gup, vec)


def _retention_kernel(p_ref, cos_ref, sin_ref, o_ref, st_scr, o_scr, *, tc):
    L = CHUNK
    G = MXU_TILE
    n_groups = RET_WIDTH // G
    hpg = G // RET_HEAD

    @pl.when(pl.program_id(1) == 0)
    def _():
        st_scr[...] = jnp.zeros_like(st_scr)

    W = RET_WIDTH
    p = p_ref[0].astype(F32)
    cos, sin = cos_ref[...], sin_ref[...]

    def rope(x):
        return jnp.concatenate(
            [x[:, h * RET_HEAD:(h + 1) * RET_HEAD] * cos
             + pltpu.roll(x[:, h * RET_HEAD:(h + 1) * RET_HEAD], RET_HEAD // 2, 1) * sin
             for h in range(RET_HEADS)], axis=1)

    q = rope(p[:, 0:W])
    k = rope(p[:, W:2 * W]) * (RET_HEAD ** -0.5)
    v = p[:, 2 * W:3 * W]
    gate = p[:, 3 * W:]

    def lane_gamma(gi):
        lg = [math.log1p(-2.0 ** (-5.0 - (gi * hpg + h))) for h in range(hpg)]
        out = jnp.full((1, G), lg[-1], F32)
        for h in range(hpg - 1, -1, -1):
            out = jnp.where(_iota((1, G), 1) < (h + 1) * RET_HEAD, lg[h], out)
        return out

    rel = (_iota((L, G), 0) - _off((L, G), 1, L)).astype(F32)
    pos = _iota((L, G), 0).astype(F32)
    bd_mask = _blk((G, G), 0, RET_HEAD) == _blk((G, G), 1, RET_HEAD)

    def bd(x):
        return jnp.where(bd_mask, _tile_rows(x.astype(BF16), hpg), jnp.zeros((), BF16))

    for gi in range(n_groups):
        ls = slice(gi * G, (gi + 1) * G)
        lg = lane_gamma(gi)
        decay = jnp.where(rel >= 0, jnp.exp(lg * jnp.maximum(rel, 0.0)), 0.0)
        q_dec = jnp.exp(lg * (pos + 1.0))
        k_dec = jnp.exp(lg * (L - 1.0 - pos))
        c_dec = jnp.exp(lg * float(L))
        c_rows = jnp.concatenate([jnp.broadcast_to(c_dec[:, h * RET_HEAD:h * RET_HEAD + 1], (RET_HEAD, 1))
                                  for h in range(hpg)], axis=0)
        st = st_scr[gi]
        for c in range(tc // L):
            sl = (slice(c * L, (c + 1) * L), ls)
            s = _bdot_nt(q[sl], bd(k[sl])) * decay
            o = _bdot(s, bd(v[sl])) + _bdot(q[sl], st) * q_dec
            o_scr[sl] = o
            st = st * c_rows + jnp.where(bd_mask, _bdot_tn(k[sl] * k_dec, v[sl]), 0.0)
        st_scr[gi] = st

    o = o_scr[...]
    outs = []
    for h in range(RET_HEADS):
        oh = o[:, h * RET_HEAD:(h + 1) * RET_HEAD]
        outs.append(oh * lax.rsqrt(jnp.mean(oh * oh, axis=-1, keepdims=True) + NORM_EPS))
    o = jnp.concatenate(outs, axis=1)
    o_ref[0] = (o * (gate * _sigmoid(gate))).astype(o_ref.dtype)


def _retention(p, cos, sin, tc):
    b, t, _ = p.shape
    return pl.pallas_call(
        functools.partial(_retention_kernel, tc=tc),
        grid=(b, t // tc),
        in_specs=[pl.BlockSpec((1, tc, RET_COLS), lambda i, j: (i, j, 0)),
                  pl.BlockSpec((tc, RET_HEAD), lambda i, j: (j, 0)),
                  pl.BlockSpec((tc, RET_HEAD), lambda i, j: (j, 0))],
        out_specs=pl.BlockSpec((1, tc, RET_WIDTH), lambda i, j: (i, j, 0)),
        out_shape=jax.ShapeDtypeStruct((b, t, RET_WIDTH), BF16),
        scratch_shapes=[pltpu.VMEM((RET_WIDTH // MXU_TILE, MXU_TILE, MXU_TILE), F32),
                        pltpu.VMEM((tc, RET_WIDTH), F32)],
        compiler_params=_params(2), name="retention",
    )(p, cos, sin)


def _mlstm_kernel(qk_ref, v_ref, og_ref, gt_ref, cw_ref, vec_ref, gb_ref, o_ref,
                  prev_scr, c_scr, m_scr, h_scr, *, tc):
    L = CHUNK
    G = MXU_TILE
    hpg = G // ML_V
    GK = hpg * ML_QK
    n_groups = ML_HEADS // hpg
    n_chunks = tc // L
    QW = ML_HEADS * ML_QK

    @pl.when(pl.program_id(1) == 0)
    def _():
        prev_scr[...] = jnp.zeros_like(prev_scr)
        c_scr[...] = jnp.zeros_like(c_scr)
        m_scr[...] = jnp.zeros_like(m_scr)

    conv_b, norm_w = vec_ref[0:1], vec_ref[1:2]
    x = qk_ref[0].astype(F32)
    prev8 = prev_scr[...]
    prev_scr[...] = x[tc - 8:tc]
    acc = x * cw_ref[ML_CONV - 1:ML_CONV] + conv_b
    for j in range(ML_CONV - 1):
        acc = acc + _shift_rows(x, prev8, ML_CONV - 1 - j) * cw_ref[j:j + 1]
    qk = acc * _sigmoid(acc)
    q, k = qk[:, :QW], qk[:, QW:] * (ML_QK ** -0.5)
    v = v_ref[0].astype(F32)

    pre = gt_ref[0].astype(F32) + gb_ref[...]
    cap = GATE_SOFTCAP * jnp.tanh(pre * (1.0 / GATE_SOFTCAP))
    li_all = cap
    lf_all = pltpu.roll(jnp.minimum(cap, 0.0) - jnp.log(1.0 + jnp.exp(-jnp.abs(cap))),
                        GATE_PAD - ML_HEADS, 1)

    tri = (_iota((L, L), 1) <= _iota((L, L), 0)).astype(BF16)
    causal = _iota((L, L), 1) <= _iota((L, L), 0)
    rows = _iota((L, 1), 0)
    k_mask = _blk((G, GK), 0, L) == _blk((G, GK), 1, ML_QK)
    v_mask = _blk((G, G), 0, L) == _blk((G, G), 1, ML_V)
    c_mask = _blk((GK, G), 0, ML_QK) == _blk((GK, G), 1, ML_V)
    c_mask2 = jnp.concatenate([c_mask, c_mask], axis=1)
    ones_bd = v_mask.astype(BF16)
    lane_k = _iota((1, GK), 1)
    row_k = _iota((GK, 1), 0)

    units = [(c, gi) for c in range(n_chunks) for gi in range(n_groups)]
    rs = {c: slice(c * L, (c + 1) * L) for c in range(n_chunks)}
    kl = {gi: slice(gi * GK, (gi + 1) * GK) for gi in range(n_groups)}
    vl = {gi: slice(gi * G, (gi + 1) * G) for gi in range(n_groups)}

    s_raw = {}
    v_bd = {}
    for (c, gi) in units:
        k_bd = jnp.where(k_mask, _tile_rows(k[rs[c], kl[gi]].astype(BF16), hpg), jnp.zeros((), BF16))
        v_bd[c, gi] = jnp.concatenate(
            [jnp.where(v_mask, _tile_rows(v[rs[c], vl[gi]].astype(BF16), hpg), jnp.zeros((), BF16)), ones_bd],
            axis=1)
        s_raw[c, gi] = _bdot_nt(q[rs[c], kl[gi]], k_bd)

    m_row = m_scr[0:1, :]
    gate = {}
    for c in range(n_chunks):
        li = li_all[rs[c]]
        b = _xdot_l(tri, lf_all[rs[c]], 3)
        beta = li - b
        cm = beta
        sh = 1
        while sh < L:
            cm = jnp.maximum(cm, jnp.where(rows >= sh, pltpu.roll(cm, sh, 0), -jnp.inf))
            sh *= 2
        mu = jnp.maximum(cm, m_row)
        mu_end = mu[L - 1:L]
        gate[c] = dict(beta_t=(beta * LOG2E).T, mu2=mu * LOG2E, inter=jnp.exp(m_row - mu),
                       emt=jnp.exp(-(b + mu)), kwf=jnp.exp(beta - mu_end))
        m_row = b[L - 1:L] + mu_end
    m_scr[0:1, :] = m_row

    s_dec, num, kv = {}, {}, {}
    for (c, gi) in units:
        gt = gate[c]
        dms = []
        for h in range(hpg):
            j = gi * hpg + h
            dms.append(jnp.where(causal, jnp.exp2(gt["beta_t"][j:j + 1, :] - gt["mu2"][:, j:j + 1]), 0.0))
        s_dec[c, gi] = s_raw[c, gi] * jnp.concatenate(dms, axis=1)
        kwf = gt["kwf"][:, gi * hpg + hpg - 1:gi * hpg + hpg]
        for h in range(hpg - 2, -1, -1):
            kwf = jnp.where(lane_k < (h + 1) * ML_QK, gt["kwf"][:, gi * hpg + h:gi * hpg + h + 1], kwf)
        kw = k[rs[c], kl[gi]] * kwf
        num[c, gi] = _bdot(s_dec[c, gi], v_bd[c, gi])
        v_one = jnp.concatenate([v[rs[c], vl[gi]].astype(BF16), jnp.ones((L, G), BF16)], axis=1)
        kv[c, gi] = jnp.where(c_mask2, _bdot_tn(kw, v_one), 0.0)

    c_st = [c_scr[gi] for gi in range(n_groups)]
    for (c, gi) in units:
        gt = gate[c]
        q_g = q[rs[c], kl[gi]]
        qc = _bdot(q_g, c_st[gi])
        cs = [gt["inter"][L - 1:L, gi * hpg + h:gi * hpg + h + 1] for h in range(hpg)]
        inter = jnp.concatenate([jnp.broadcast_to(gt["inter"][:, gi * hpg + h:gi * hpg + h + 1], (L, ML_V))
                                 for h in range(hpg)], axis=1)
        emt = jnp.concatenate([jnp.broadcast_to(gt["emt"][:, gi * hpg + h:gi * hpg + h + 1], (L, ML_V))
                               for h in range(hpg)], axis=1)
        den = num[c, gi][:, G:] + inter * qc[:, G:]
        rden = 1.0 / jnp.maximum(jnp.abs(den), emt)
        h_scr[rs[c], vl[gi]] = (num[c, gi][:, :G] + inter * qc[:, :G]) * rden
        cs_rows = cs[hpg - 1]
        for h in range(hpg - 2, -1, -1):
            cs_rows = jnp.where(row_k < (h + 1) * ML_QK, cs[h], cs_rows)
        c_st[gi] = cs_rows * c_st[gi] + kv[c, gi]
    for gi in range(n_groups):
        c_scr[gi] = c_st[gi]

    hfull = h_scr[...]
    outs = []
    for h in range(ML_HEADS):
        hh = hfull[:, h * ML_V:(h + 1) * ML_V]
        outs.append(hh * lax.rsqrt(jnp.mean(hh * hh, axis=-1, keepdims=True) + NORM_EPS))
    hn = jnp.concatenate(outs, axis=1) * norm_w
    o_ref[0] = (hn * _sigmoid(og_ref[0].astype(F32))).astype(o_ref.dtype)


def _mlstm(qk, v, og, gt, conv_w, vec, gbias, tc):
    b, t, _ = qk.shape
    hpg = MXU_TILE // ML_V
    n_groups = ML_HEADS // hpg
    tile = lambda w: pl.BlockSpec((1, tc, w), lambda i, j: (i, j, 0))
    return pl.pallas_call(
        functools.partial(_mlstm_kernel, tc=tc),
        grid=(b, t // tc),
        in_specs=[tile(D_MODEL), tile(D_MODEL), tile(D_MODEL), tile(GATE_PAD),
                  _const_spec(conv_w.shape), _const_spec(vec.shape), _const_spec(gbias.shape)],
        out_specs=tile(D_MODEL),
        out_shape=jax.ShapeDtypeStruct((b, t, D_MODEL), BF16),
        scratch_shapes=[pltpu.VMEM((8, D_MODEL), F32),
                        pltpu.VMEM((n_groups, hpg * ML_QK, 2 * MXU_TILE), F32),
                        pltpu.VMEM((8, LANES), F32),
                        pltpu.VMEM((tc, D_MODEL), F32)],
        compiler_params=_params(2), name="mlstm",
    )(qk, v, og, gt, conv_w, vec, gbias)


def kernel(x, norm_mix_g, norm_mlp_g, norm_final_g, ab_w_in, rwkv_mu, rwkv_w0, rwkv_w_up, rwkv_a0,
           rwkv_a_up, rwkv_g_up, rwkv_k_k, rwkv_k_a, rwkv_r_k, rwkv_ln_w, rwkv_ln_b, ab_w_out,
           c_w_in, c_conv_w, c_conv_b, c_i_bias, c_f_bias, c_norm_w, c_w_out, mlp_w1, mlp_w2):
    bsz, seq, d = x.shape
    n = bsz * seq
    tm = 512
    tc = 256
    bf = lambda w: w.astype(BF16)
    x2d = x.reshape(n, d)

    w_in = ab_w_in[0]
    p_a, p_b = _norm_proj(x2d, norm_mix_g[0], [bf(w_in[:, :RWKV_COLS]), bf(w_in[:, RWKV_COLS:])], tm)
    half = RWKV_LORA // 2
    zeros = jnp.zeros((half, RWKV_WIDTH), F32)
    wa = bf(jnp.concatenate([jnp.concatenate([rwkv_w_up[0], zeros], axis=1),
                             jnp.concatenate([zeros, rwkv_a_up[0]], axis=1)], axis=0))
    vec = jnp.stack([rwkv_w0[0], rwkv_a0[0], rwkv_k_k[0], rwkv_k_a[0], rwkv_r_k[0].reshape(-1),
                     rwkv_ln_w[0], rwkv_ln_b[0], jnp.zeros((RWKV_WIDTH,), F32)])
    y_a = _rwkv(p_a.reshape(bsz, seq, RWKV_COLS), rwkv_mu[0].reshape(1, RWKV_COLS), wa,
                bf(rwkv_g_up[0]), vec, tc)

    pos = jnp.arange(seq, dtype=F32)
    inv = ROPE_BASE ** (-jnp.arange(0, RET_HEAD, 2, dtype=F32) / RET_HEAD)
    ang = pos[:, None] * inv[None, :]
    cos, sin = jnp.cos(ang), jnp.sin(ang)
    y_b = _retention(p_b.reshape(bsz, seq, RET_COLS), jnp.concatenate([cos, cos], axis=1),
                     jnp.concatenate([-sin, sin], axis=1), tc)

    w_out = bf(ab_w_out[0])
    x2d = _mix_mlp([y_a.reshape(n, RWKV_WIDTH), y_b.reshape(n, RET_WIDTH)], x2d, norm_mlp_g[0],
                   norm_final_g, [w_out[:RWKV_WIDTH], w_out[RWKV_WIDTH:]], bf(mlp_w1[0]), bf(mlp_w2[0]),
                   tm, final=False)

    w_in = c_w_in[0]
    qk_cols = 2 * ML_HEADS * ML_QK
    w_g = jnp.pad(w_in[:, qk_cols + 2 * d:], ((0, 0), (0, GATE_PAD - 2 * ML_HEADS)))
    qk, v, og, gt = _norm_proj(x2d, norm_mix_g[1],
                               [bf(w_in[:, :qk_cols]), bf(w_in[:, qk_cols:qk_cols + d]),
                                bf(w_in[:, qk_cols + d:qk_cols + 2 * d]), bf(w_g)], tm)
    gbias = jnp.pad(jnp.concatenate([c_i_bias[0], c_f_bias[0]]), (0, GATE_PAD - 2 * ML_HEADS)).reshape(1, GATE_PAD)
    vec = jnp.stack([c_conv_b[0], c_norm_w[0]] + [jnp.zeros((d,), F32)] * 6)
    shp = lambda a: a.reshape(bsz, seq, a.shape[-1])
    y_c = _mlstm(shp(qk), shp(v), shp(og), shp(gt), c_conv_w[0], vec, gbias, tc)

    x2d = _mix_mlp([y_c.reshape(n, d)], x2d, norm_mlp_g[1], norm_final_g, [bf(c_w_out[0])],
                   bf(mlp_w1[1]), bf(mlp_w2[1]), tm, final=True)
    return x2d.reshape(bsz, seq, d)
```

```python
import functools
import math

import jax
import jax.numpy as jnp
from jax import lax
from jax.experimental import pallas as pl
from jax.experimental.pallas import tpu as pltpu

F32 = jnp.float32
BF16 = jnp.bfloat16

D_MODEL = 1024
D_FF = 4 * D_MODEL
NORM_EPS = 1e-6

RWKV_WIDTH = 512
RWKV_HEAD = 64
RWKV_LORA = 128
RWKV_GATE_LORA = 128
RWKV_COLS = 3 * RWKV_WIDTH + RWKV_LORA + RWKV_GATE_LORA
RWKV_GN_EPS = 64e-5
RWKV_CHUNK = 64

RET_WIDTH = 512
RET_HEADS = 4
RET_HEAD = 128
RET_COLS = 4 * RET_WIDTH
ROPE_BASE = 10000.0
CHUNK = 128

ML_HEADS = 8
ML_QK = 64
ML_V = 128
ML_CONV = 4
GATE_SOFTCAP = 15.0
LOG2E = math.log2(math.e)
GATE_PAD = 128

MLP_TILE = 512
RWKV_TILE = 256
RET_TILE = 512
ML_TILE = 256

LANES = 128
MXU_TILE = 256
VMEM_LIMIT = 56 * 1024 * 1024


def _iota(shape, dim):
    return lax.broadcasted_iota(jnp.int32, shape, dim)


def _blk(shape, dim, n):
    return lax.shift_right_logical(_iota(shape, dim), int(math.log2(n)))


def _off(shape, dim, n):
    return _iota(shape, dim) & (n - 1)


def _bdot(a, b):
    return jnp.dot(a.astype(BF16), b.astype(BF16), preferred_element_type=F32)


def _bdot_nt(a, b):
    return lax.dot_general(a.astype(BF16), b.astype(BF16), (((1,), (1,)), ((), ())),
                           preferred_element_type=F32)


def _bdot_tn(a, b):
    return lax.dot_general(a.astype(BF16), b.astype(BF16), (((0,), (0,)), ((), ())),
                           preferred_element_type=F32)


def _split(x, n):
    parts = []
    rem = x
    for _ in range(n):
        p = rem.astype(BF16)
        parts.append(p)
        rem = rem - p.astype(F32)
    return parts


def _xdot_l(m01, x, n=2):
    return sum(jnp.dot(m01, p, preferred_element_type=F32) for p in _split(x, n))


def _xdot_r(x, m01, n=2):
    return sum(jnp.dot(p, m01, preferred_element_type=F32) for p in _split(x, n))


def _sigmoid(x):
    return 1.0 / (1.0 + jnp.exp(-x))


def _rms(x, g):
    return x * lax.rsqrt(jnp.mean(x * x, axis=-1, keepdims=True) + NORM_EPS) * g


def _tile_rows(x, reps):
    return jnp.concatenate([x] * reps, axis=0)


def _shift_rows(x, prev8, k):
    rolled = pltpu.roll(x, k, 0)
    head = jnp.where(_iota((8, 1), 0) < k, pltpu.roll(prev8, k, 0), rolled[0:8])
    return jnp.concatenate([head, rolled[8:]], axis=0)


def _const_spec(shape):
    return pl.BlockSpec(shape, lambda *_: (0,) * len(shape))


def _params(n_grid):
    return pltpu.CompilerParams(dimension_semantics=("arbitrary",) * n_grid,
                                vmem_limit_bytes=VMEM_LIMIT)


def _project_tile(x_ref, g_ref, w_ref, p_scr, n_blocks):
    @pl.when(pl.program_id(0) == 0)
    def _():
        p_scr[...] = jnp.zeros_like(p_scr)

    p_prev = p_scr[...]
    h = _rms(x_ref[...], g_ref[...]).astype(BF16)
    width = w_ref.shape[1] // n_blocks

    def emit(i):
        cols = slice(i * width, (i + 1) * width)
        p_scr[:, cols] = jnp.dot(h, w_ref[:, cols], preferred_element_type=F32)

    return p_prev, emit


def _first_tile_of_row(nt):
    return lax.rem(jnp.maximum(pl.program_id(0) - 1, 0), nt) == 0


def _mixer_call(kernel_fn, x2d, g, w, consts, out_width, tc, nt, scratch, name, time_tables=()):
    n = x2d.shape[0]
    total = n // tc
    once = pl.Buffered(1)
    in_specs = [pl.BlockSpec((tc, D_MODEL), lambda s: (jnp.minimum(s, total - 1), 0)),
                _const_spec((1, D_MODEL)),
                pl.BlockSpec(w.shape, lambda s: (0, 0), pipeline_mode=once)]
    in_specs += [pl.BlockSpec((tc, t.shape[1]), lambda s: (lax.rem(jnp.maximum(s - 1, 0), nt), 0))
                 for t in time_tables]
    in_specs += [_const_spec(c.shape) for c in consts]
    return pl.pallas_call(
        kernel_fn, grid=(total + 1,), in_specs=in_specs,
        out_specs=pl.BlockSpec((tc, out_width), lambda s: (jnp.maximum(s - 1, 0), 0)),
        out_shape=jax.ShapeDtypeStruct((n, out_width), BF16),
        scratch_shapes=[pltpu.VMEM((tc, w.shape[1]), F32)] + scratch,
        compiler_params=_params(1), name=name,
    )(x2d, g.reshape(1, D_MODEL), w, *time_tables, *consts)


def _mix_mlp_kernel(*refs, n_y, final):
    y_refs, rest = refs[:n_y], refs[n_y:]
    x_ref, g_ref, gf_ref = rest[0], rest[1], rest[2]
    wo_refs, (w1_ref, w2_ref, o_ref) = rest[3:3 + n_y], rest[3 + n_y:]
    x1 = x_ref[...]
    for y_ref, wo_ref in zip(y_refs, wo_refs):
        x1 = x1 + jnp.dot(y_ref[...], wo_ref[...], preferred_element_type=F32)
    h = _rms(x1, g_ref[...]).astype(BF16)
    u = jnp.dot(h, w1_ref[...], preferred_element_type=F32)
    u = jnp.square(jnp.maximum(u, 0.0)).astype(BF16)
    x2 = x1 + jnp.dot(u, w2_ref[...], preferred_element_type=F32)
    if final:
        x2 = _rms(x2, gf_ref[...])
    o_ref[...] = x2


def _mix_mlp(ys, x2d, g, gf, w_outs, w1, w2, tm, final):
    n = x2d.shape[0]
    once = pl.Buffered(1)
    in_specs = [pl.BlockSpec((tm, y.shape[1]), lambda i: (i, 0)) for y in ys]
    in_specs += [pl.BlockSpec((tm, D_MODEL), lambda i: (i, 0)),
                 _const_spec((1, D_MODEL)), _const_spec((1, D_MODEL))]
    in_specs += [pl.BlockSpec(w.shape, lambda i: (0, 0), pipeline_mode=once) for w in w_outs]
    in_specs += [pl.BlockSpec(w1.shape, lambda i: (0, 0), pipeline_mode=once),
                 pl.BlockSpec(w2.shape, lambda i: (0, 0), pipeline_mode=once)]
    return pl.pallas_call(
        functools.partial(_mix_mlp_kernel, n_y=len(ys), final=final),
        grid=(n // tm,), in_specs=in_specs,
        out_specs=pl.BlockSpec((tm, D_MODEL), lambda i: (i, 0)),
        out_shape=jax.ShapeDtypeStruct((n, D_MODEL), F32),
        compiler_params=_params(1), name="mix_mlp",
    )(*ys, x2d, g.reshape(1, D_MODEL), gf.reshape(1, D_MODEL), *w_outs, w1, w2)


def _rwkv_kernel(x_ref, g_ref, w_ref, mu_ref, wa_ref, gup_ref, vec_ref, o_ref,
                 p_scr, prev_scr, st_scr, y_scr, *, tc, nt):
    T = RWKV_CHUNK
    G = MXU_TILE
    n_groups = RWKV_WIDTH // G
    n_chunks = tc // T

    @pl.when(_first_tile_of_row(nt))
    def _():
        prev_scr[...] = jnp.zeros_like(prev_scr)
        st_scr[...] = jnp.zeros_like(st_scr)

    p_raw, project = _project_tile(x_ref, g_ref, w_ref, p_scr, 2)
    shifted = _shift_rows(p_raw, prev_scr[...], 1)
    prev_scr[...] = p_raw[tc - 8:tc]
    p = p_raw + (shifted - p_raw) * mu_ref[...]

    w0, a0, k_k, k_a = (vec_ref[i:i + 1] for i in range(4))
    r_k, ln_w, ln_b = (vec_ref[i:i + 1] for i in range(4, 7))
    W = RWKV_WIDTH
    r, k, v = p[:, 0:W], p[:, W:2 * W], p[:, 2 * W:3 * W]
    lora = p[:, 3 * W:3 * W + RWKV_LORA]
    g_lo = p[:, 3 * W + RWKV_LORA:]

    lora = jnp.where(_iota((1, RWKV_LORA), 1) < RWKV_LORA // 2, jnp.tanh(lora), lora)
    za = _bdot(lora, wa_ref[...])
    ld = (-math.exp(-0.5) * LOG2E) * _sigmoid(w0 + za[:, :W])
    a = _sigmoid(a0 + za[:, W:])
    g = _bdot(_sigmoid(g_lo), gup_ref[...])
    project(0)

    ones_g = (_blk((G, G), 0, RWKV_HEAD) == _blk((G, G), 1, RWKV_HEAD)).astype(BF16)

    def head_sum(x):
        return jnp.concatenate([_bdot(x[:, i * G:(i + 1) * G], ones_g) for i in range(n_groups)], axis=1)

    kk = k * k_k
    kk = kk * lax.rsqrt(jnp.maximum(head_sum(kk * kk), 1e-24))
    k = k * (1.0 + (a - 1.0) * k_a)
    a_s = -kk
    b_s = kk * a
    bonus = head_sum(r * k * r_k) * v

    ri, ci = _iota((tc, tc), 0), _iota((tc, tc), 1)
    same = _blk((tc, tc), 0, T) == _blk((tc, tc), 1, T)
    L = _xdot_l((same & (ci <= ri)).astype(BF16), ld)
    LT = jnp.concatenate([jnp.broadcast_to(L[(c + 1) * T - 1:(c + 1) * T], (T, W))
                          for c in range(n_chunks)], axis=0)
    e_neg = jnp.exp2(-L)
    e_end = jnp.exp2(LT - L)
    rt = r * jnp.exp2(L)
    at = a_s * jnp.exp2(L - ld)
    bt, kt = b_s * e_neg, k * e_neg
    bh, kh = b_s * e_end, k * e_end
    w_end = jnp.exp2(LT)

    row, col = _iota((T, G), 0), _off((T, G), 1, T)
    strict, incl = col < row, col <= row
    eye_cat = (col == row).astype(F32)
    bd_mask = _blk((G, G), 0, T) == _blk((G, G), 1, T)
    eye_g = _iota((G, G), 0) == _iota((G, G), 1)

    def bd(x):
        return jnp.where(bd_mask, _tile_rows(x.astype(BF16), G // T), jnp.zeros((), BF16))

    units = [(gi, c) for c in range(n_chunks) for gi in range(n_groups)]
    sl = {u: (slice(u[1] * T, (u[1] + 1) * T), slice(u[0] * G, (u[0] + 1) * G)) for u in units}
    lhs = {u: jnp.concatenate([at[sl[u]], rt[sl[u]]], axis=0) for u in units}
    a1 = {u: _bdot_nt(lhs[u], bd(bt[sl[u]])) for u in units}
    a2 = {u: _bdot_nt(lhs[u], bd(kt[sl[u]])) for u in units}
    a_ab = {u: jnp.where(strict, a1[u][:T], 0.0) for u in units}
    a_rb = {u: jnp.where(incl, a1[u][T:], 0.0) for u in units}
    a_ak = {u: jnp.where(strict, a2[u][:T], 0.0) for u in units}
    a_rk = {u: jnp.where(incl, a2[u][T:], 0.0) for u in units}
    acc = {u: eye_cat + a_ab[u] for u in units}
    pw = {u: _bdot(a_ab[u], bd(a_ab[u])) for u in units}
    av = {u: _bdot(a_ak[u], bd(v[sl[u]])) for u in units}
    for _ in range(int(math.log2(T)) - 2):
        both = {u: _bdot(jnp.concatenate([pw[u], acc[u]], axis=0), bd(pw[u])) for u in units}
        pw = {u: both[u][:T] for u in units}
        acc = {u: acc[u] + both[u][T:] for u in units}
    tinv = {u: acc[u] + _bdot(acc[u], bd(pw[u])) for u in units}
    pu = {u: _bdot(tinv[u], jnp.concatenate([bd(at[sl[u]]), bd(av[u])], axis=1)) for u in units}
    p_c = {u: pu[u][:, :G] for u in units}
    u0 = {u: pu[u][:, G:] for u in units}
    m_bd = {u: jnp.where(bd_mask, _bdot_tn(bh[sl[u]], p_c[u]), 0.0)
            + jnp.where(eye_g, w_end[u[1] * T:u[1] * T + 1, sl[u][1]], 0.0) for u in units}
    n_bd = {u: jnp.where(bd_mask, _bdot_tn(jnp.concatenate([bh[sl[u]], kh[sl[u]]], axis=0),
                                           jnp.concatenate([u0[u], v[sl[u]]], axis=0)), 0.0)
            for u in units}
    q_c = {u: rt[sl[u]] + _bdot(a_rb[u], bd(p_c[u])) for u in units}
    y0 = {u: _bdot(jnp.concatenate([a_rb[u], a_rk[u]], axis=1),
                   jnp.concatenate([bd(u0[u]), bd(v[sl[u]])], axis=0)) for u in units}
    st = [st_scr[gi] for gi in range(n_groups)]
    for u in units:
        gi = u[0]
        y_scr[sl[u]] = _bdot(q_c[u], st[gi]) + y0[u]
        st[gi] = _bdot(m_bd[u], st[gi]) + n_bd[u]
    for gi in range(n_groups):
        st_scr[gi] = st[gi]
    project(1)

    y = y_scr[...]
    mean = head_sum(y) * (1.0 / RWKV_HEAD)
    yc = y - mean
    var = head_sum(yc * yc) * (1.0 / RWKV_HEAD)
    y = yc * lax.rsqrt(var + RWKV_GN_EPS) * ln_w + ln_b
    o_ref[...] = ((y + bonus) * g).astype(o_ref.dtype)


def _rwkv(x2d, g, w, mu, wa, gup, vec, tc, nt):
    scratch = [pltpu.VMEM((8, RWKV_COLS), F32),
               pltpu.VMEM((RWKV_WIDTH // MXU_TILE, MXU_TILE, MXU_TILE), F32),
               pltpu.VMEM((tc, RWKV_WIDTH), F32)]
    return _mixer_call(functools.partial(_rwkv_kernel, tc=tc, nt=nt), x2d, g, w, [mu, wa, gup, vec],
                       RWKV_WIDTH, tc, nt, scratch, "rwkv7")


def _retention_kernel(x_ref, g_ref, w_ref, cos_ref, sin_ref, o_ref, p_scr, st_scr, o_scr, *, tc, nt):
    L = CHUNK
    G = MXU_TILE
    n_groups = RET_WIDTH // G
    hpg = G // RET_HEAD

    @pl.when(_first_tile_of_row(nt))
    def _():
        st_scr[...] = jnp.zeros_like(st_scr)

    W = RET_WIDTH
    p, project = _project_tile(x_ref, g_ref, w_ref, p_scr, 2)
    project(0)
    cos, sin = cos_ref[...], sin_ref[...]

    def rope(x):
        return jnp.concatenate(
            [x[:, h * RET_HEAD:(h + 1) * RET_HEAD] * cos
             + pltpu.roll(x[:, h * RET_HEAD:(h + 1) * RET_HEAD], RET_HEAD // 2, 1) * sin
             for h in range(RET_HEADS)], axis=1)

    q = rope(p[:, 0:W])
    k = rope(p[:, W:2 * W]) * (RET_HEAD ** -0.5)
    v = p[:, 2 * W:3 * W]
    gate = p[:, 3 * W:]

    def lane_gamma(gi):
        lg = [math.log1p(-2.0 ** (-5.0 - (gi * hpg + h))) for h in range(hpg)]
        out = jnp.full((1, G), lg[-1], F32)
        for h in range(hpg - 1, -1, -1):
            out = jnp.where(_iota((1, G), 1) < (h + 1) * RET_HEAD, lg[h], out)
        return out

    rel = (_iota((L, G), 0) - _off((L, G), 1, L)).astype(F32)
    pos = _iota((L, G), 0).astype(F32)
    bd_mask = _blk((G, G), 0, RET_HEAD) == _blk((G, G), 1, RET_HEAD)

    def bd(x):
        return jnp.where(bd_mask, _tile_rows(x.astype(BF16), hpg), jnp.zeros((), BF16))

    for gi in range(n_groups):
        ls = slice(gi * G, (gi + 1) * G)
        lg = lane_gamma(gi)
        decay = jnp.where(rel >= 0, jnp.exp(lg * jnp.maximum(rel, 0.0)), 0.0)
        q_dec = jnp.exp(lg * (pos + 1.0))
        k_dec = jnp.exp(lg * (L - 1.0 - pos))
        c_dec = jnp.exp(lg * float(L))
        c_rows = jnp.concatenate([jnp.broadcast_to(c_dec[:, h * RET_HEAD:h * RET_HEAD + 1], (RET_HEAD, 1))
                                  for h in range(hpg)], axis=0)
        st = st_scr[gi]
        for c in range(tc // L):
            sl = (slice(c * L, (c + 1) * L), ls)
            s = _bdot_nt(q[sl], bd(k[sl])) * decay
            o = _bdot(s, bd(v[sl])) + _bdot(q[sl], st) * q_dec
            o_scr[sl] = o
            st = st * c_rows + jnp.where(bd_mask, _bdot_tn(k[sl] * k_dec, v[sl]), 0.0)
        st_scr[gi] = st
    project(1)

    o = o_scr[...]
    outs = []
    for h in range(RET_HEADS):
        oh = o[:, h * RET_HEAD:(h + 1) * RET_HEAD]
        outs.append(oh * lax.rsqrt(jnp.mean(oh * oh, axis=-1, keepdims=True) + NORM_EPS))
    o = jnp.concatenate(outs, axis=1)
    o_ref[...] = (o * (gate * _sigmoid(gate))).astype(o_ref.dtype)


def _retention(x2d, g, w, cos, sin, tc, nt):
    scratch = [pltpu.VMEM((RET_WIDTH // MXU_TILE, MXU_TILE, MXU_TILE), F32),
               pltpu.VMEM((tc, RET_WIDTH), F32)]
    return _mixer_call(functools.partial(_retention_kernel, tc=tc, nt=nt), x2d, g, w, [],
                       RET_WIDTH, tc, nt, scratch, "retention", time_tables=[cos, sin])


def _mlstm_kernel(x_ref, g_ref, w_ref, cw_ref, vec_ref, gb_ref, o_ref,
                  p_scr, prev_scr, c_scr, m_scr, h_scr, *, tc, nt):
    L = CHUNK
    G = MXU_TILE
    hpg = G // ML_V
    GK = hpg * ML_QK
    n_groups = ML_HEADS // hpg
    n_chunks = tc // L
    QW = ML_HEADS * ML_QK

    @pl.when(_first_tile_of_row(nt))
    def _():
        prev_scr[...] = jnp.zeros_like(prev_scr)
        c_scr[...] = jnp.zeros_like(c_scr)
        m_scr[...] = jnp.zeros_like(m_scr)

    p, project = _project_tile(x_ref, g_ref, w_ref, p_scr, 5)
    D = D_MODEL
    conv_b, norm_w = vec_ref[0:1], vec_ref[1:2]
    x = p[:, :D]
    prev8 = prev_scr[...]
    prev_scr[...] = x[tc - 8:tc]
    acc = x * cw_ref[ML_CONV - 1:ML_CONV] + conv_b
    for j in range(ML_CONV - 1):
        acc = acc + _shift_rows(x, prev8, ML_CONV - 1 - j) * cw_ref[j:j + 1]
    qk = acc * _sigmoid(acc)
    q, k = qk[:, :QW], qk[:, QW:] * (ML_QK ** -0.5)
    v = p[:, D:2 * D]
    project(0)

    pre = p[:, 3 * D:] + gb_ref[...]
    cap = GATE_SOFTCAP * jnp.tanh(pre * (1.0 / GATE_SOFTCAP))
    li_all = cap
    lf_all = pltpu.roll(jnp.minimum(cap, 0.0) - jnp.log(1.0 + jnp.exp(-jnp.abs(cap))),
                        GATE_PAD - ML_HEADS, 1)

    tri = (_iota((L, L), 1) <= _iota((L, L), 0)).astype(BF16)
    causal = _iota((L, L), 1) <= _iota((L, L), 0)
    rows = _iota((L, 1), 0)
    k_mask = _blk((G, GK), 0, L) == _blk((G, GK), 1, ML_QK)
    v_mask = _blk((G, G), 0, L) == _blk((G, G), 1, ML_V)
    c_mask = _blk((GK, G), 0, ML_QK) == _blk((GK, G), 1, ML_V)
    c_mask2 = jnp.concatenate([c_mask, c_mask], axis=1)
    ones_bd = v_mask.astype(BF16)
    lane_k = _iota((1, GK), 1)
    row_k = _iota((GK, 1), 0)

    units = [(c, gi) for c in range(n_chunks) for gi in range(n_groups)]
    rs = {c: slice(c * L, (c + 1) * L) for c in range(n_chunks)}
    kl = {gi: slice(gi * GK, (gi + 1) * GK) for gi in range(n_groups)}
    vl = {gi: slice(gi * G, (gi + 1) * G) for gi in range(n_groups)}

    s_raw = {}
    v_bd = {}
    for (c, gi) in units:
        k_bd = jnp.where(k_mask, _tile_rows(k[rs[c], kl[gi]].astype(BF16), hpg), jnp.zeros((), BF16))
        v_bd[c, gi] = jnp.concatenate(
            [jnp.where(v_mask, _tile_rows(v[rs[c], vl[gi]].astype(BF16), hpg), jnp.zeros((), BF16)), ones_bd],
            axis=1)
        s_raw[c, gi] = _bdot_nt(q[rs[c], kl[gi]], k_bd)

    project(1)

    m_row = m_scr[0:1, :]
    gate = {}
    for c in range(n_chunks):
        li = li_all[rs[c]]
        b = _xdot_l(tri, lf_all[rs[c]], 3)
        beta = li - b
        cm = beta
        sh = 1
        while sh < L:
            cm = jnp.maximum(cm, jnp.where(rows >= sh, pltpu.roll(cm, sh, 0), -jnp.inf))
            sh *= 2
        mu = jnp.maximum(cm, m_row)
        mu_end = mu[L - 1:L]
        gate[c] = dict(beta_t=(beta * LOG2E).T, mu2=mu * LOG2E, inter=jnp.exp(m_row - mu),
                       emt=jnp.exp(-(b + mu)), kwf=jnp.exp(beta - mu_end))
        m_row = b[L - 1:L] + mu_end
    m_scr[0:1, :] = m_row
    project(2)

    s_dec, num, kv = {}, {}, {}
    for (c, gi) in units:
        gt = gate[c]
        dms = []
        for h in range(hpg):
            j = gi * hpg + h
            dms.append(jnp.where(causal, jnp.exp2(gt["beta_t"][j:j + 1, :] - gt["mu2"][:, j:j + 1]), 0.0))
        s_dec[c, gi] = s_raw[c, gi] * jnp.concatenate(dms, axis=1)
        kwf = gt["kwf"][:, gi * hpg + hpg - 1:gi * hpg + hpg]
        for h in range(hpg - 2, -1, -1):
            kwf = jnp.where(lane_k < (h + 1) * ML_QK, gt["kwf"][:, gi * hpg + h:gi * hpg + h + 1], kwf)
        kw = k[rs[c], kl[gi]] * kwf
        num[c, gi] = _bdot(s_dec[c, gi], v_bd[c, gi])
        v_one = jnp.concatenate([v[rs[c], vl[gi]].astype(BF16), jnp.ones((L, G), BF16)], axis=1)
        kv[c, gi] = jnp.where(c_mask2, _bdot_tn(kw, v_one), 0.0)

    project(3)

    c_st = [c_scr[gi] for gi in range(n_groups)]
    for (c, gi) in units:
        gt = gate[c]
        q_g = q[rs[c], kl[gi]]
        qc = _bdot(q_g, c_st[gi])
        cs = [gt["inter"][L - 1:L, gi * hpg + h:gi * hpg + h + 1] for h in range(hpg)]
        inter = jnp.concatenate([jnp.broadcast_to(gt["inter"][:, gi * hpg + h:gi * hpg + h + 1], (L, ML_V))
                                 for h in range(hpg)], axis=1)
        emt = jnp.concatenate([jnp.broadcast_to(gt["emt"][:, gi * hpg + h:gi * hpg + h + 1], (L, ML_V))
                               for h in range(hpg)], axis=1)
        den = num[c, gi][:, G:] + inter * qc[:, G:]
        rden = 1.0 / jnp.maximum(jnp.abs(den), emt)
        h_scr[rs[c], vl[gi]] = (num[c, gi][:, :G] + inter * qc[:, :G]) * rden
        cs_rows = cs[hpg - 1]
        for h in range(hpg - 2, -1, -1):
            cs_rows = jnp.where(row_k < (h + 1) * ML_QK, cs[h], cs_rows)
        c_st[gi] = cs_rows * c_st[gi] + kv[c, gi]
    for gi in range(n_groups):
        c_scr[gi] = c_st[gi]
    project(4)

    hfull = h_scr[...]
    outs = []
    for h in range(ML_HEADS):
        hh = hfull[:, h * ML_V:(h + 1) * ML_V]
        outs.append(hh * lax.rsqrt(jnp.mean(hh * hh, axis=-1, keepdims=True) + NORM_EPS))
    hn = jnp.concatenate(outs, axis=1) * norm_w
    o_ref[...] = (hn * _sigmoid(p[:, 2 * D:3 * D])).astype(o_ref.dtype)


def _mlstm(x2d, g, w, conv_w, vec, gbias, tc, nt):
    hpg = MXU_TILE // ML_V
    scratch = [pltpu.VMEM((8, D_MODEL), F32),
               pltpu.VMEM((ML_HEADS // hpg, hpg * ML_QK, 2 * MXU_TILE), F32),
               pltpu.VMEM((8, LANES), F32),
               pltpu.VMEM((tc, D_MODEL), F32)]
    return _mixer_call(functools.partial(_mlstm_kernel, tc=tc, nt=nt), x2d, g, w, [conv_w, vec, gbias],
                       D_MODEL, tc, nt, scratch, "mlstm")


def kernel(x, norm_mix_g, norm_mlp_g, norm_final_g, ab_w_in, rwkv_mu, rwkv_w0, rwkv_w_up, rwkv_a0,
           rwkv_a_up, rwkv_g_up, rwkv_k_k, rwkv_k_a, rwkv_r_k, rwkv_ln_w, rwkv_ln_b, ab_w_out,
           c_w_in, c_conv_w, c_conv_b, c_i_bias, c_f_bias, c_norm_w, c_w_out, mlp_w1, mlp_w2):
    bsz, seq, d = x.shape
    n = bsz * seq
    tm = MLP_TILE
    bf = lambda w: w.astype(BF16)
    x2d = x.reshape(n, d)

    w_in = ab_w_in[0]
    half = RWKV_LORA // 2
    zeros = jnp.zeros((half, RWKV_WIDTH), F32)
    wa = bf(jnp.concatenate([jnp.concatenate([rwkv_w_up[0], zeros], axis=1),
                             jnp.concatenate([zeros, rwkv_a_up[0]], axis=1)], axis=0))
    vec = jnp.stack([rwkv_w0[0], rwkv_a0[0], rwkv_k_k[0], rwkv_k_a[0], rwkv_r_k[0].reshape(-1),
                     rwkv_ln_w[0], rwkv_ln_b[0], jnp.zeros((RWKV_WIDTH,), F32)])
    y_a = _rwkv(x2d, norm_mix_g[0], bf(w_in[:, :RWKV_COLS]), rwkv_mu[0].reshape(1, RWKV_COLS), wa,
                bf(rwkv_g_up[0]), vec, RWKV_TILE, seq // RWKV_TILE)

    pos = jnp.arange(seq, dtype=F32)
    inv = ROPE_BASE ** (-jnp.arange(0, RET_HEAD, 2, dtype=F32) / RET_HEAD)
    ang = pos[:, None] * inv[None, :]
    cos, sin = jnp.cos(ang), jnp.sin(ang)
    y_b = _retention(x2d, norm_mix_g[0], bf(w_in[:, RWKV_COLS:]), jnp.concatenate([cos, cos], axis=1),
                     jnp.concatenate([-sin, sin], axis=1), RET_TILE, seq // RET_TILE)

    w_out = bf(ab_w_out[0])
    x2d = _mix_mlp([y_a, y_b], x2d, norm_mlp_g[0],
                   norm_final_g, [w_out[:RWKV_WIDTH], w_out[RWKV_WIDTH:]], bf(mlp_w1[0]), bf(mlp_w2[0]),
                   tm, final=False)

    w_in = jnp.pad(c_w_in[0], ((0, 0), (0, GATE_PAD - 2 * ML_HEADS)))
    gbias = jnp.pad(jnp.concatenate([c_i_bias[0], c_f_bias[0]]), (0, GATE_PAD - 2 * ML_HEADS)).reshape(1, GATE_PAD)
    vec = jnp.stack([c_conv_b[0], c_norm_w[0]] + [jnp.zeros((d,), F32)] * 6)
    y_c = _mlstm(x2d, norm_mix_g[1], bf(w_in), c_conv_w[0], vec, gbias, ML_TILE, seq // ML_TILE)

    x2d = _mix_mlp([y_c], x2d, norm_mlp_g[1], norm_final_g, [bf(c_w_out[0])],
                   bf(mlp_w1[1]), bf(mlp_w2[1]), tm, final=True)
    return x2d.reshape(bsz, seq, d)
```

```python
import functools
import math

import jax
import jax.numpy as jnp
from jax import lax
from jax.experimental import pallas as pl
from jax.experimental.pallas import tpu as pltpu

F32 = jnp.float32
BF16 = jnp.bfloat16

D_MODEL = 1024
D_FF = 4 * D_MODEL
NORM_EPS = 1e-6

RWKV_WIDTH = 512
RWKV_HEAD = 64
RWKV_LORA = 128
RWKV_GATE_LORA = 128
RWKV_COLS = 3 * RWKV_WIDTH + RWKV_LORA + RWKV_GATE_LORA
RWKV_GN_EPS = 64e-5
RWKV_CHUNK = 64

RET_WIDTH = 512
RET_HEADS = 4
RET_HEAD = 128
RET_COLS = 4 * RET_WIDTH
ROPE_BASE = 10000.0
CHUNK = 128

ML_HEADS = 8
ML_QK = 64
ML_V = 128
ML_CONV = 4
GATE_SOFTCAP = 15.0
LOG2E = math.log2(math.e)
GATE_PAD = 128

MLP_TILE = 512
RWKV_TILE = 512
RET_TILE = 512
ML_TILE = 512

LANES = 128
MXU_TILE = 256
VMEM_LIMIT = 56 * 1024 * 1024


def _iota(shape, dim):
    return lax.broadcasted_iota(jnp.int32, shape, dim)


def _blk(shape, dim, n):
    return lax.shift_right_logical(_iota(shape, dim), int(math.log2(n)))


def _off(shape, dim, n):
    return _iota(shape, dim) & (n - 1)


def _bdot(a, b):
    return jnp.dot(a.astype(BF16), b.astype(BF16), preferred_element_type=F32)


def _bdot_nt(a, b):
    return lax.dot_general(a.astype(BF16), b.astype(BF16), (((1,), (1,)), ((), ())),
                           preferred_element_type=F32)


def _bdot_tn(a, b):
    return lax.dot_general(a.astype(BF16), b.astype(BF16), (((0,), (0,)), ((), ())),
                           preferred_element_type=F32)


def _split(x, n):
    parts = []
    rem = x
    for _ in range(n):
        p = rem.astype(BF16)
        parts.append(p)
        rem = rem - p.astype(F32)
    return parts


def _xdot_l(m01, x, n=2):
    return sum(jnp.dot(m01, p, preferred_element_type=F32) for p in _split(x, n))


def _xdot_r(x, m01, n=2):
    return sum(jnp.dot(p, m01, preferred_element_type=F32) for p in _split(x, n))


def _sigmoid(x):
    return 1.0 / (1.0 + jnp.exp(-x))


def _rms(x, g):
    return x * lax.rsqrt(jnp.mean(x * x, axis=-1, keepdims=True) + NORM_EPS) * g


def _tile_rows(x, reps):
    return jnp.concatenate([x] * reps, axis=0)


def _shift_rows(x, prev8, k):
    rolled = pltpu.roll(x, k, 0)
    head = jnp.where(_iota((8, 1), 0) < k, pltpu.roll(prev8, k, 0), rolled[0:8])
    return jnp.concatenate([head, rolled[8:]], axis=0)


def _const_spec(shape):
    return pl.BlockSpec(shape, lambda *_: (0,) * len(shape))


def _params(n_grid):
    return pltpu.CompilerParams(dimension_semantics=("arbitrary",) * n_grid,
                                vmem_limit_bytes=VMEM_LIMIT)


def _project_tile(x_ref, g_ref, w_ref, p_scr, n_blocks):
    @pl.when(pl.program_id(0) == 0)
    def _():
        p_scr[...] = jnp.zeros_like(p_scr)

    p_prev = p_scr[...]
    h = _rms(x_ref[...], g_ref[...]).astype(BF16)
    width = w_ref.shape[1] // n_blocks

    def emit(i):
        cols = slice(i * width, (i + 1) * width)
        p_scr[:, cols] = jnp.dot(h, w_ref[:, cols], preferred_element_type=F32)

    return p_prev, emit


def _first_tile_of_row(nt):
    return lax.rem(jnp.maximum(pl.program_id(0) - 1, 0), nt) == 0


def _mixer_call(kernel_fn, x2d, g, w, consts, out_width, tc, nt, scratch, name, time_tables=()):
    n = x2d.shape[0]
    total = n // tc
    once = pl.Buffered(1)
    in_specs = [pl.BlockSpec((tc, D_MODEL), lambda s: (jnp.minimum(s, total - 1), 0)),
                _const_spec((1, D_MODEL)),
                pl.BlockSpec(w.shape, lambda s: (0, 0), pipeline_mode=once)]
    in_specs += [pl.BlockSpec((tc, t.shape[1]), lambda s: (lax.rem(jnp.maximum(s - 1, 0), nt), 0))
                 for t in time_tables]
    in_specs += [_const_spec(c.shape) for c in consts]
    return pl.pallas_call(
        kernel_fn, grid=(total + 1,), in_specs=in_specs,
        out_specs=pl.BlockSpec((tc, out_width), lambda s: (jnp.maximum(s - 1, 0), 0)),
        out_shape=jax.ShapeDtypeStruct((n, out_width), BF16),
        scratch_shapes=[pltpu.VMEM((tc, w.shape[1]), F32)] + scratch,
        compiler_params=_params(1), name=name,
    )(x2d, g.reshape(1, D_MODEL), w, *time_tables, *consts)


def _mix_mlp_kernel(*refs, n_y, final):
    y_refs, rest = refs[:n_y], refs[n_y:]
    x_ref, g_ref, gf_ref = rest[0], rest[1], rest[2]
    wo_refs, (w1_ref, w2_ref, o_ref) = rest[3:3 + n_y], rest[3 + n_y:]
    x1 = x_ref[...]
    for y_ref, wo_ref in zip(y_refs, wo_refs):
        x1 = x1 + jnp.dot(y_ref[...], wo_ref[...], preferred_element_type=F32)
    h = _rms(x1, g_ref[...]).astype(BF16)
    u = jnp.dot(h, w1_ref[...], preferred_element_type=F32)
    u = jnp.square(jnp.maximum(u, 0.0)).astype(BF16)
    x2 = x1 + jnp.dot(u, w2_ref[...], preferred_element_type=F32)
    if final:
        x2 = _rms(x2, gf_ref[...])
    o_ref[...] = x2


def _mix_mlp(ys, x2d, g, gf, w_outs, w1, w2, tm, final):
    n = x2d.shape[0]
    once = pl.Buffered(1)
    in_specs = [pl.BlockSpec((tm, y.shape[1]), lambda i: (i, 0)) for y in ys]
    in_specs += [pl.BlockSpec((tm, D_MODEL), lambda i: (i, 0)),
                 _const_spec((1, D_MODEL)), _const_spec((1, D_MODEL))]
    in_specs += [pl.BlockSpec(w.shape, lambda i: (0, 0), pipeline_mode=once) for w in w_outs]
    in_specs += [pl.BlockSpec(w1.shape, lambda i: (0, 0), pipeline_mode=once),
                 pl.BlockSpec(w2.shape, lambda i: (0, 0), pipeline_mode=once)]
    return pl.pallas_call(
        functools.partial(_mix_mlp_kernel, n_y=len(ys), final=final),
        grid=(n // tm,), in_specs=in_specs,
        out_specs=pl.BlockSpec((tm, D_MODEL), lambda i: (i, 0)),
        out_shape=jax.ShapeDtypeStruct((n, D_MODEL), F32),
        compiler_params=_params(1), name="mix_mlp",
    )(*ys, x2d, g.reshape(1, D_MODEL), gf.reshape(1, D_MODEL), *w_outs, w1, w2)


def _rwkv_kernel(x_ref, g_ref, w_ref, mu_ref, wa_ref, gup_ref, vec_ref, o_ref,
                 p_scr, prev_scr, st_scr, y_scr, *, tc, nt):
    T = RWKV_CHUNK
    G = MXU_TILE
    n_groups = RWKV_WIDTH // G
    n_chunks = tc // T

    @pl.when(_first_tile_of_row(nt))
    def _():
        prev_scr[...] = jnp.zeros_like(prev_scr)
        st_scr[...] = jnp.zeros_like(st_scr)

    n_proj = RWKV_COLS // MXU_TILE
    p_raw, project = _project_tile(x_ref, g_ref, w_ref, p_scr, n_proj)
    shifted = _shift_rows(p_raw, prev_scr[...], 1)
    prev_scr[...] = p_raw[tc - 8:tc]
    p = p_raw + (shifted - p_raw) * mu_ref[...]

    w0, a0, k_k, k_a = (vec_ref[i:i + 1] for i in range(4))
    r_k, ln_w, ln_b = (vec_ref[i:i + 1] for i in range(4, 7))
    W = RWKV_WIDTH
    r, k, v = p[:, 0:W], p[:, W:2 * W], p[:, 2 * W:3 * W]
    lora = p[:, 3 * W:3 * W + RWKV_LORA]
    g_lo = p[:, 3 * W + RWKV_LORA:]

    lora = jnp.where(_iota((1, RWKV_LORA), 1) < RWKV_LORA // 2, jnp.tanh(lora), lora)
    za = _bdot(lora, wa_ref[...])
    ld = (-math.exp(-0.5) * LOG2E) * _sigmoid(w0 + za[:, :W])
    a = _sigmoid(a0 + za[:, W:])
    g = _bdot(_sigmoid(g_lo), gup_ref[...])
    project(0)

    ones_g = (_blk((G, G), 0, RWKV_HEAD) == _blk((G, G), 1, RWKV_HEAD)).astype(BF16)

    def head_sum(x):
        return jnp.concatenate([_bdot(x[:, i * G:(i + 1) * G], ones_g) for i in range(n_groups)], axis=1)

    kk = k * k_k
    kk = kk * lax.rsqrt(jnp.maximum(head_sum(kk * kk), 1e-24))
    project(1)
    k = k * (1.0 + (a - 1.0) * k_a)
    a_s = -kk
    b_s = kk * a
    bonus = head_sum(r * k * r_k) * v
    project(2)

    ri, ci = _iota((tc, tc), 0), _iota((tc, tc), 1)
    same = _blk((tc, tc), 0, T) == _blk((tc, tc), 1, T)
    L = _xdot_l((same & (ci <= ri)).astype(BF16), ld)
    project(3)
    LT = jnp.concatenate([jnp.broadcast_to(L[(c + 1) * T - 1:(c + 1) * T], (T, W))
                          for c in range(n_chunks)], axis=0)
    e_neg = jnp.exp2(-L)
    e_end = jnp.exp2(LT - L)
    rt = r * jnp.exp2(L)
    at = a_s * jnp.exp2(L - ld)
    bt, kt = b_s * e_neg, k * e_neg
    bh, kh = b_s * e_end, k * e_end
    w_end = jnp.exp2(LT)

    row, col = _iota((T, G), 0), _off((T, G), 1, T)
    strict, incl = col < row, col <= row
    eye_cat = (col == row).astype(F32)
    bd_mask = _blk((G, G), 0, T) == _blk((G, G), 1, T)
    eye_g = _iota((G, G), 0) == _iota((G, G), 1)

    def bd(x):
        return jnp.where(bd_mask, _tile_rows(x.astype(BF16), G // T), jnp.zeros((), BF16))

    units = [(gi, c) for c in range(n_chunks) for gi in range(n_groups)]
    sl = {u: (slice(u[1] * T, (u[1] + 1) * T), slice(u[0] * G, (u[0] + 1) * G)) for u in units}
    lhs = {u: jnp.concatenate([at[sl[u]], rt[sl[u]]], axis=0) for u in units}
    a1 = {u: _bdot_nt(lhs[u], bd(bt[sl[u]])) for u in units}
    a2 = {u: _bdot_nt(lhs[u], bd(kt[sl[u]])) for u in units}
    a_ab = {u: jnp.where(strict, a1[u][:T], 0.0) for u in units}
    a_rb = {u: jnp.where(incl, a1[u][T:], 0.0) for u in units}
    a_ak = {u: jnp.where(strict, a2[u][:T], 0.0) for u in units}
    a_rk = {u: jnp.where(incl, a2[u][T:], 0.0) for u in units}
    acc = {u: eye_cat + a_ab[u] for u in units}
    pw = {u: _bdot(a_ab[u], bd(a_ab[u])) for u in units}
    av = {u: _bdot(a_ak[u], bd(v[sl[u]])) for u in units}
    for _ in range(int(math.log2(T)) - 2):
        both = {u: _bdot(jnp.concatenate([pw[u], acc[u]], axis=0), bd(pw[u])) for u in units}
        pw = {u: both[u][:T] for u in units}
        acc = {u: acc[u] + both[u][T:] for u in units}
    tinv = {u: acc[u] + _bdot(acc[u], bd(pw[u])) for u in units}
    pu = {u: _bdot(tinv[u], jnp.concatenate([bd(at[sl[u]]), bd(av[u])], axis=1)) for u in units}
    p_c = {u: pu[u][:, :G] for u in units}
    u0 = {u: pu[u][:, G:] for u in units}
    m_bd = {u: jnp.where(bd_mask, _bdot_tn(bh[sl[u]], p_c[u]), 0.0)
            + jnp.where(eye_g, w_end[u[1] * T:u[1] * T + 1, sl[u][1]], 0.0) for u in units}
    n_bd = {u: jnp.where(bd_mask, _bdot_tn(jnp.concatenate([bh[sl[u]], kh[sl[u]]], axis=0),
                                           jnp.concatenate([u0[u], v[sl[u]]], axis=0)), 0.0)
            for u in units}
    q_c = {u: rt[sl[u]] + _bdot(a_rb[u], bd(p_c[u])) for u in units}
    y0 = {u: _bdot(jnp.concatenate([a_rb[u], a_rk[u]], axis=1),
                   jnp.concatenate([bd(u0[u]), bd(v[sl[u]])], axis=0)) for u in units}
    st = [st_scr[gi] for gi in range(n_groups)]
    blocks = list(range(4, n_proj))
    for u in units:
        gi, c = u
        y_scr[sl[u]] = _bdot(q_c[u], st[gi]) + y0[u]
        st[gi] = _bdot(m_bd[u], st[gi]) + n_bd[u]
        if gi == n_groups - 1 and c + 1 < n_chunks:
            for _ in range(-(-len(blocks) // (n_chunks - 1 - c))):
                project(blocks.pop(0))
    for gi in range(n_groups):
        st_scr[gi] = st[gi]

    y = y_scr[...]
    mean = head_sum(y) * (1.0 / RWKV_HEAD)
    yc = y - mean
    var = head_sum(yc * yc) * (1.0 / RWKV_HEAD)
    y = yc * lax.rsqrt(var + RWKV_GN_EPS) * ln_w + ln_b
    o_ref[...] = ((y + bonus) * g).astype(o_ref.dtype)


def _rwkv(x2d, g, w, mu, wa, gup, vec, tc, nt):
    scratch = [pltpu.VMEM((8, RWKV_COLS), F32),
               pltpu.VMEM((RWKV_WIDTH // MXU_TILE, MXU_TILE, MXU_TILE), F32),
               pltpu.VMEM((tc, RWKV_WIDTH), F32)]
    return _mixer_call(functools.partial(_rwkv_kernel, tc=tc, nt=nt), x2d, g, w, [mu, wa, gup, vec],
                       RWKV_WIDTH, tc, nt, scratch, "rwkv7")


def _retention_kernel(x_ref, g_ref, w_ref, cos_ref, sin_ref, o_ref, p_scr, st_scr, o_scr, *, tc, nt):
    L = CHUNK
    G = MXU_TILE
    n_groups = RET_WIDTH // G
    hpg = G // RET_HEAD

    @pl.when(_first_tile_of_row(nt))
    def _():
        st_scr[...] = jnp.zeros_like(st_scr)

    W = RET_WIDTH
    p, project = _project_tile(x_ref, g_ref, w_ref, p_scr, 2)
    project(0)
    cos, sin = cos_ref[...], sin_ref[...]

    def rope(x):
        return jnp.concatenate(
            [x[:, h * RET_HEAD:(h + 1) * RET_HEAD] * cos
             + pltpu.roll(x[:, h * RET_HEAD:(h + 1) * RET_HEAD], RET_HEAD // 2, 1) * sin
             for h in range(RET_HEADS)], axis=1)

    q = rope(p[:, 0:W])
    k = rope(p[:, W:2 * W]) * (RET_HEAD ** -0.5)
    v = p[:, 2 * W:3 * W]
    gate = p[:, 3 * W:]

    def lane_gamma(gi):
        lg = [math.log1p(-2.0 ** (-5.0 - (gi * hpg + h))) for h in range(hpg)]
        out = jnp.full((1, G), lg[-1], F32)
        for h in range(hpg - 1, -1, -1):
            out = jnp.where(_iota((1, G), 1) < (h + 1) * RET_HEAD, lg[h], out)
        return out

    rel = (_iota((L, G), 0) - _off((L, G), 1, L)).astype(F32)
    pos = _iota((L, G), 0).astype(F32)
    bd_mask = _blk((G, G), 0, RET_HEAD) == _blk((G, G), 1, RET_HEAD)

    def bd(x):
        return jnp.where(bd_mask, _tile_rows(x.astype(BF16), hpg), jnp.zeros((), BF16))

    for gi in range(n_groups):
        ls = slice(gi * G, (gi + 1) * G)
        lg = lane_gamma(gi)
        decay = jnp.where(rel >= 0, jnp.exp(lg * jnp.maximum(rel, 0.0)), 0.0)
        q_dec = jnp.exp(lg * (pos + 1.0))
        k_dec = jnp.exp(lg * (L - 1.0 - pos))
        c_dec = jnp.exp(lg * float(L))
        c_rows = jnp.concatenate([jnp.broadcast_to(c_dec[:, h * RET_HEAD:h * RET_HEAD + 1], (RET_HEAD, 1))
                                  for h in range(hpg)], axis=0)
        st = st_scr[gi]
        for c in range(tc // L):
            sl = (slice(c * L, (c + 1) * L), ls)
            s = _bdot_nt(q[sl], bd(k[sl])) * decay
            o = _bdot(s, bd(v[sl])) + _bdot(q[sl], st) * q_dec
            o_scr[sl] = o
            st = st * c_rows + jnp.where(bd_mask, _bdot_tn(k[sl] * k_dec, v[sl]), 0.0)
        st_scr[gi] = st
    project(1)

    o = o_scr[...]
    outs = []
    for h in range(RET_HEADS):
        oh = o[:, h * RET_HEAD:(h + 1) * RET_HEAD]
        outs.append(oh * lax.rsqrt(jnp.mean(oh * oh, axis=-1, keepdims=True) + NORM_EPS))
    o = jnp.concatenate(outs, axis=1)
    o_ref[...] = (o * (gate * _sigmoid(gate))).astype(o_ref.dtype)


def _retention(x2d, g, w, cos, sin, tc, nt):
    scratch = [pltpu.VMEM((RET_WIDTH // MXU_TILE, MXU_TILE, MXU_TILE), F32),
               pltpu.VMEM((tc, RET_WIDTH), F32)]
    return _mixer_call(functools.partial(_retention_kernel, tc=tc, nt=nt), x2d, g, w, [],
                       RET_WIDTH, tc, nt, scratch, "retention", time_tables=[cos, sin])


def _mlstm_kernel(x_ref, g_ref, w_ref, cw_ref, vec_ref, gb_ref, o_ref,
                  p_scr, prev_scr, c_scr, m_scr, h_scr, *, tc, nt):
    L = CHUNK
    G = MXU_TILE
    hpg = G // ML_V
    GK = hpg * ML_QK
    n_groups = ML_HEADS // hpg
    n_chunks = tc // L
    QW = ML_HEADS * ML_QK

    @pl.when(_first_tile_of_row(nt))
    def _():
        prev_scr[...] = jnp.zeros_like(prev_scr)
        c_scr[...] = jnp.zeros_like(c_scr)
        m_scr[...] = jnp.zeros_like(m_scr)

    p, project = _project_tile(x_ref, g_ref, w_ref, p_scr, 5)
    D = D_MODEL
    conv_b, norm_w = vec_ref[0:1], vec_ref[1:2]
    x = p[:, :D]
    prev8 = prev_scr[...]
    prev_scr[...] = x[tc - 8:tc]
    acc = x * cw_ref[ML_CONV - 1:ML_CONV] + conv_b
    for j in range(ML_CONV - 1):
        acc = acc + _shift_rows(x, prev8, ML_CONV - 1 - j) * cw_ref[j:j + 1]
    qk = acc * _sigmoid(acc)
    q, k = qk[:, :QW], qk[:, QW:] * (ML_QK ** -0.5)
    v = p[:, D:2 * D]
    project(0)

    pre = p[:, 3 * D:] + gb_ref[...]
    cap = GATE_SOFTCAP * jnp.tanh(pre * (1.0 / GATE_SOFTCAP))
    li_all = cap
    lf_all = pltpu.roll(jnp.minimum(cap, 0.0) - jnp.log(1.0 + jnp.exp(-jnp.abs(cap))),
                        GATE_PAD - ML_HEADS, 1)

    tri = (_iota((L, L), 1) <= _iota((L, L), 0)).astype(BF16)
    causal = _iota((L, L), 1) <= _iota((L, L), 0)
    rows = _iota((L, 1), 0)
    k_mask = _blk((G, GK), 0, L) == _blk((G, GK), 1, ML_QK)
    v_mask = _blk((G, G), 0, L) == _blk((G, G), 1, ML_V)
    c_mask = _blk((GK, G), 0, ML_QK) == _blk((GK, G), 1, ML_V)
    c_mask2 = jnp.concatenate([c_mask, c_mask], axis=1)
    ones_bd = v_mask.astype(BF16)
    lane_k = _iota((1, GK), 1)
    row_k = _iota((GK, 1), 0)

    units = [(c, gi) for c in range(n_chunks) for gi in range(n_groups)]
    rs = {c: slice(c * L, (c + 1) * L) for c in range(n_chunks)}
    kl = {gi: slice(gi * GK, (gi + 1) * GK) for gi in range(n_groups)}
    vl = {gi: slice(gi * G, (gi + 1) * G) for gi in range(n_groups)}

    s_raw = {}
    v_bd = {}
    for (c, gi) in units:
        k_bd = jnp.where(k_mask, _tile_rows(k[rs[c], kl[gi]].astype(BF16), hpg), jnp.zeros((), BF16))
        v_bd[c, gi] = jnp.concatenate(
            [jnp.where(v_mask, _tile_rows(v[rs[c], vl[gi]].astype(BF16), hpg), jnp.zeros((), BF16)), ones_bd],
            axis=1)
        s_raw[c, gi] = _bdot_nt(q[rs[c], kl[gi]], k_bd)

    project(1)

    m_row = m_scr[0:1, :]
    gate = {}
    for c in range(n_chunks):
        li = li_all[rs[c]]
        b = _xdot_l(tri, lf_all[rs[c]], 3)
        beta = li - b
        cm = beta
        sh = 1
        while sh < L:
            cm = jnp.maximum(cm, jnp.where(rows >= sh, pltpu.roll(cm, sh, 0), -jnp.inf))
            sh *= 2
        mu = jnp.maximum(cm, m_row)
        mu_end = mu[L - 1:L]
        gate[c] = dict(beta_t=(beta * LOG2E).T, mu2=mu * LOG2E, inter=jnp.exp(m_row - mu),
                       emt=jnp.exp(-(b + mu)), kwf=jnp.exp(beta - mu_end))
        m_row = b[L - 1:L] + mu_end
    m_scr[0:1, :] = m_row
    project(2)

    s_dec, num, kv = {}, {}, {}
    for (c, gi) in units:
        gt = gate[c]
        dms = []
        for h in range(hpg):
            j = gi * hpg + h
            dms.append(jnp.where(causal, jnp.exp2(gt["beta_t"][j:j + 1, :] - gt["mu2"][:, j:j + 1]), 0.0))
        s_dec[c, gi] = s_raw[c, gi] * jnp.concatenate(dms, axis=1)
        kwf = gt["kwf"][:, gi * hpg + hpg - 1:gi * hpg + hpg]
        for h in range(hpg - 2, -1, -1):
            kwf = jnp.where(lane_k < (h + 1) * ML_QK, gt["kwf"][:, gi * hpg + h:gi * hpg + h + 1], kwf)
        kw = k[rs[c], kl[gi]] * kwf
        num[c, gi] = _bdot(s_dec[c, gi], v_bd[c, gi])
        v_one = jnp.concatenate([v[rs[c], vl[gi]].astype(BF16), jnp.ones((L, G), BF16)], axis=1)
        kv[c, gi] = jnp.where(c_mask2, _bdot_tn(kw, v_one), 0.0)

    project(3)

    c_st = [c_scr[gi] for gi in range(n_groups)]
    for (c, gi) in units:
        gt = gate[c]
        q_g = q[rs[c], kl[gi]]
        qc = _bdot(q_g, c_st[gi])
        cs = [gt["inter"][L - 1:L, gi * hpg + h:gi * hpg + h + 1] for h in range(hpg)]
        inter = jnp.concatenate([jnp.broadcast_to(gt["inter"][:, gi * hpg + h:gi * hpg + h + 1], (L, ML_V))
                                 for h in range(hpg)], axis=1)
        emt = jnp.concatenate([jnp.broadcast_to(gt["emt"][:, gi * hpg + h:gi * hpg + h + 1], (L, ML_V))
                               for h in range(hpg)], axis=1)
        den = num[c, gi][:, G:] + inter * qc[:, G:]
        rden = 1.0 / jnp.maximum(jnp.abs(den), emt)
        h_scr[rs[c], vl[gi]] = (num[c, gi][:, :G] + inter * qc[:, :G]) * rden
        cs_rows = cs[hpg - 1]
        for h in range(hpg - 2, -1, -1):
            cs_rows = jnp.where(row_k < (h + 1) * ML_QK, cs[h], cs_rows)
        c_st[gi] = cs_rows * c_st[gi] + kv[c, gi]
    for gi in range(n_groups):
        c_scr[gi] = c_st[gi]
    project(4)

    hfull = h_scr[...]
    outs = []
    for h in range(ML_HEADS):
        hh = hfull[:, h * ML_V:(h + 1) * ML_V]
        outs.append(hh * lax.rsqrt(jnp.mean(hh * hh, axis=-1, keepdims=True) + NORM_EPS))
    hn = jnp.concatenate(outs, axis=1) * norm_w
    o_ref[...] = (hn * _sigmoid(p[:, 2 * D:3 * D])).astype(o_ref.dtype)


def _mlstm(x2d, g, w, conv_w, vec, gbias, tc, nt):
    hpg = MXU_TILE // ML_V
    scratch = [pltpu.VMEM((8, D_MODEL), F32),
               pltpu.VMEM((ML_HEADS // hpg, hpg * ML_QK, 2 * MXU_TILE), F32),
               pltpu.VMEM((8, LANES), F32),
               pltpu.VMEM((tc, D_MODEL), F32)]
    return _mixer_call(functools.partial(_mlstm_kernel, tc=tc, nt=nt), x2d, g, w, [conv_w, vec, gbias],
                       D_MODEL, tc, nt, scratch, "mlstm")


def kernel(x, norm_mix_g, norm_mlp_g, norm_final_g, ab_w_in, rwkv_mu, rwkv_w0, rwkv_w_up, rwkv_a0,
           rwkv_a_up, rwkv_g_up, rwkv_k_k, rwkv_k_a, rwkv_r_k, rwkv_ln_w, rwkv_ln_b, ab_w_out,
           c_w_in, c_conv_w, c_conv_b, c_i_bias, c_f_bias, c_norm_w, c_w_out, mlp_w1, mlp_w2):
    bsz, seq, d = x.shape
    n = bsz * seq
    tm = MLP_TILE
    bf = lambda w: w.astype(BF16)
    x2d = x.reshape(n, d)

    w_in = ab_w_in[0]
    half = RWKV_LORA // 2
    zeros = jnp.zeros((half, RWKV_WIDTH), F32)
    wa = bf(jnp.concatenate([jnp.concatenate([rwkv_w_up[0], zeros], axis=1),
                             jnp.concatenate([zeros, rwkv_a_up[0]], axis=1)], axis=0))
    vec = jnp.stack([rwkv_w0[0], rwkv_a0[0], rwkv_k_k[0], rwkv_k_a[0], rwkv_r_k[0].reshape(-1),
                     rwkv_ln_w[0], rwkv_ln_b[0], jnp.zeros((RWKV_WIDTH,), F32)])
    y_a = _rwkv(x2d, norm_mix_g[0], bf(w_in[:, :RWKV_COLS]), rwkv_mu[0].reshape(1, RWKV_COLS), wa,
                bf(rwkv_g_up[0]), vec, RWKV_TILE, seq // RWKV_TILE)

    pos = jnp.arange(seq, dtype=F32)
    inv = ROPE_BASE ** (-jnp.arange(0, RET_HEAD, 2, dtype=F32) / RET_HEAD)
    ang = pos[:, None] * inv[None, :]
    cos, sin = jnp.cos(ang), jnp.sin(ang)
    y_b = _retention(x2d, norm_mix_g[0], bf(w_in[:, RWKV_COLS:]), jnp.concatenate([cos, cos], axis=1),
                     jnp.concatenate([-sin, sin], axis=1), RET_TILE, seq // RET_TILE)

    w_out = bf(ab_w_out[0])
    x2d = _mix_mlp([y_a, y_b], x2d, norm_mlp_g[0],
                   norm_final_g, [w_out[:RWKV_WIDTH], w_out[RWKV_WIDTH:]], bf(mlp_w1[0]), bf(mlp_w2[0]),
                   tm, final=False)

    w_in = jnp.pad(c_w_in[0], ((0, 0), (0, GATE_PAD - 2 * ML_HEADS)))
    gbias = jnp.pad(jnp.concatenate([c_i_bias[0], c_f_bias[0]]), (0, GATE_PAD - 2 * ML_HEADS)).reshape(1, GATE_PAD)
    vec = jnp.stack([c_conv_b[0], c_norm_w[0]] + [jnp.zeros((d,), F32)] * 6)
    y_c = _mlstm(x2d, norm_mix_g[1], bf(w_in), c_conv_w[0], vec, gbias, ML_TILE, seq // ML_TILE)

    x2d = _mix_mlp([y_c], x2d, norm_mlp_g[1], norm_final_g, [bf(c_w_out[0])],
                   bf(mlp_w1[1]), bf(mlp_w2[1]), tm, final=True)
    return x2d.reshape(bsz, seq, d)
```

```python
import functools
import math

import jax
import jax.numpy as jnp
from jax import lax
from jax.experimental import pallas as pl
from jax.experimental.pallas import tpu as pltpu

F32 = jnp.float32
BF16 = jnp.bfloat16

D_MODEL = 1024
D_FF = 4 * D_MODEL
NORM_EPS = 1e-6

RWKV_WIDTH = 512
RWKV_HEAD = 64
RWKV_LORA = 128
RWKV_GATE_LORA = 128
RWKV_COLS = 3 * RWKV_WIDTH + RWKV_LORA + RWKV_GATE_LORA
RWKV_GN_EPS = 64e-5
RWKV_CHUNK = 64

RET_WIDTH = 512
RET_HEADS = 4
RET_HEAD = 128
RET_COLS = 4 * RET_WIDTH
ROPE_BASE = 10000.0
CHUNK = 128

ML_HEADS = 8
ML_QK = 64
ML_V = 128
ML_CONV = 4
GATE_SOFTCAP = 15.0
LOG2E = math.log2(math.e)
GATE_PAD = 128

MLP_TILE = 512
RWKV_TILE = 512
ML_TILE = 512

LANES = 128
MXU_TILE = 256
VMEM_LIMIT = 56 * 1024 * 1024


def _iota(shape, dim):
    return lax.broadcasted_iota(jnp.int32, shape, dim)


def _blk(shape, dim, n):
    return lax.shift_right_logical(_iota(shape, dim), int(math.log2(n)))


def _off(shape, dim, n):
    return _iota(shape, dim) & (n - 1)


def _bdot(a, b):
    return jnp.dot(a.astype(BF16), b.astype(BF16), preferred_element_type=F32)


def _bdot_nt(a, b):
    return lax.dot_general(a.astype(BF16), b.astype(BF16), (((1,), (1,)), ((), ())),
                           preferred_element_type=F32)


def _bdot_tn(a, b):
    return lax.dot_general(a.astype(BF16), b.astype(BF16), (((0,), (0,)), ((), ())),
                           preferred_element_type=F32)


def _split(x, n):
    parts = []
    rem = x
    for _ in range(n):
        p = rem.astype(BF16)
        parts.append(p)
        rem = rem - p.astype(F32)
    return parts


def _xdot_l(m01, x, n=2):
    return sum(jnp.dot(m01, p, preferred_element_type=F32) for p in _split(x, n))


def _xdot_r(x, m01, n=2):
    return sum(jnp.dot(p, m01, preferred_element_type=F32) for p in _split(x, n))


def _sigmoid(x):
    return 1.0 / (1.0 + jnp.exp(-x))


def _rms(x, g):
    return x * lax.rsqrt(jnp.mean(x * x, axis=-1, keepdims=True) + NORM_EPS) * g


def _tile_rows(x, reps):
    return jnp.concatenate([x] * reps, axis=0)


def _shift_rows(x, prev8, k):
    rolled = pltpu.roll(x, k, 0)
    head = jnp.where(_iota((8, 1), 0) < k, pltpu.roll(prev8, k, 0), rolled[0:8])
    return jnp.concatenate([head, rolled[8:]], axis=0)


def _const_spec(shape):
    return pl.BlockSpec(shape, lambda *_: (0,) * len(shape))


def _params(n_grid):
    return pltpu.CompilerParams(dimension_semantics=("arbitrary",) * n_grid,
                                vmem_limit_bytes=VMEM_LIMIT)


def _project_tile(x_ref, g_ref, w_ref, p_scr, n_blocks):
    @pl.when(pl.program_id(0) == 0)
    def _():
        p_scr[...] = jnp.zeros_like(p_scr)

    p_prev = p_scr[...]
    h = _rms(x_ref[...], g_ref[...]).astype(BF16)
    width = w_ref.shape[1] // n_blocks

    def emit(i):
        cols = slice(i * width, (i + 1) * width)
        p_scr[:, cols] = jnp.dot(h, w_ref[:, cols], preferred_element_type=F32)

    return p_prev, emit


def _first_tile_of_row(nt):
    return lax.rem(jnp.maximum(pl.program_id(0) - 1, 0), nt) == 0


def _mixer_call(kernel_fn, x2d, g, w, consts, out_widths, tc, nt, scratch, name, time_tables=()):
    n = x2d.shape[0]
    total = n // tc
    once = pl.Buffered(1)
    in_specs = [pl.BlockSpec((tc, D_MODEL), lambda s: (jnp.minimum(s, total - 1), 0)),
                _const_spec((1, D_MODEL)),
                pl.BlockSpec(w.shape, lambda s: (0, 0), pipeline_mode=once)]
    in_specs += [pl.BlockSpec((tc, t.shape[1]), lambda s: (lax.rem(jnp.maximum(s - 1, 0), nt), 0))
                 for t in time_tables]
    in_specs += [_const_spec(c.shape) for c in consts]
    return pl.pallas_call(
        kernel_fn, grid=(total + 1,), in_specs=in_specs,
        out_specs=[pl.BlockSpec((tc, ow), lambda s: (jnp.maximum(s - 1, 0), 0)) for ow in out_widths],
        out_shape=[jax.ShapeDtypeStruct((n, ow), BF16) for ow in out_widths],
        scratch_shapes=[pltpu.VMEM((tc, w.shape[1]), F32)] + scratch,
        compiler_params=_params(1), name=name,
    )(x2d, g.reshape(1, D_MODEL), w, *time_tables, *consts)


def _mix_mlp_kernel(*refs, n_y, final):
    y_refs, rest = refs[:n_y], refs[n_y:]
    x_ref, g_ref, gf_ref = rest[0], rest[1], rest[2]
    wo_refs, (w1_ref, w2_ref, o_ref) = rest[3:3 + n_y], rest[3 + n_y:]
    x1 = x_ref[...]
    for y_ref, wo_ref in zip(y_refs, wo_refs):
        x1 = x1 + jnp.dot(y_ref[...], wo_ref[...], preferred_element_type=F32)
    h = _rms(x1, g_ref[...]).astype(BF16)
    u = jnp.dot(h, w1_ref[...], preferred_element_type=F32)
    u = jnp.square(jnp.maximum(u, 0.0)).astype(BF16)
    x2 = x1 + jnp.dot(u, w2_ref[...], preferred_element_type=F32)
    if final:
        x2 = _rms(x2, gf_ref[...])
    o_ref[...] = x2


def _mix_mlp(ys, x2d, g, gf, w_outs, w1, w2, tm, final):
    n = x2d.shape[0]
    once = pl.Buffered(1)
    in_specs = [pl.BlockSpec((tm, y.shape[1]), lambda i: (i, 0)) for y in ys]
    in_specs += [pl.BlockSpec((tm, D_MODEL), lambda i: (i, 0)),
                 _const_spec((1, D_MODEL)), _const_spec((1, D_MODEL))]
    in_specs += [pl.BlockSpec(w.shape, lambda i: (0, 0), pipeline_mode=once) for w in w_outs]
    in_specs += [pl.BlockSpec(w1.shape, lambda i: (0, 0), pipeline_mode=once),
                 pl.BlockSpec(w2.shape, lambda i: (0, 0), pipeline_mode=once)]
    return pl.pallas_call(
        functools.partial(_mix_mlp_kernel, n_y=len(ys), final=final),
        grid=(n // tm,), in_specs=in_specs,
        out_specs=pl.BlockSpec((tm, D_MODEL), lambda i: (i, 0)),
        out_shape=jax.ShapeDtypeStruct((n, D_MODEL), F32),
        compiler_params=_params(1), name="mix_mlp",
    )(*ys, x2d, g.reshape(1, D_MODEL), gf.reshape(1, D_MODEL), *w_outs, w1, w2)


def _layer0_kernel(x_ref, g_ref, w_ref, cos_ref, sin_ref, mu_ref, wa_ref, gup_ref, vec_ref, o_ref, ob_ref,
                   p_scr, prev_scr, st_scr, y_scr, rst_scr, ro_scr, *, tc, nt):
    T = RWKV_CHUNK
    G = MXU_TILE
    n_groups = RWKV_WIDTH // G
    n_chunks = tc // T

    @pl.when(_first_tile_of_row(nt))
    def _():
        prev_scr[...] = jnp.zeros_like(prev_scr)
        st_scr[...] = jnp.zeros_like(st_scr)
        rst_scr[...] = jnp.zeros_like(rst_scr)

    n_proj = RWKV_COLS // MXU_TILE
    p_all, project = _project_tile(x_ref, g_ref, w_ref, p_scr, (RWKV_COLS + RET_COLS) // MXU_TILE)
    p_raw = p_all[:, :RWKV_COLS]
    ret = _retention_parts(p_all[:, RWKV_COLS:], cos_ref, sin_ref, ob_ref, rst_scr, ro_scr, tc)
    ret_blocks = list(range(n_proj, (RWKV_COLS + RET_COLS) // MXU_TILE))

    def fill(n):
        for _ in range(min(n, len(ret_blocks))):
            project(ret_blocks.pop(0))
    shifted = _shift_rows(p_raw, prev_scr[...], 1)
    prev_scr[...] = p_raw[tc - 8:tc]
    p = p_raw + (shifted - p_raw) * mu_ref[...]

    w0, a0, k_k, k_a = (vec_ref[i:i + 1] for i in range(4))
    r_k, ln_w, ln_b = (vec_ref[i:i + 1] for i in range(4, 7))
    W = RWKV_WIDTH
    r, k, v = p[:, 0:W], p[:, W:2 * W], p[:, 2 * W:3 * W]
    lora = p[:, 3 * W:3 * W + RWKV_LORA]
    g_lo = p[:, 3 * W + RWKV_LORA:]

    lora = jnp.where(_iota((1, RWKV_LORA), 1) < RWKV_LORA // 2, jnp.tanh(lora), lora)
    za = _bdot(lora, wa_ref[...])
    ld = (-math.exp(-0.5) * LOG2E) * _sigmoid(w0 + za[:, :W])
    a = _sigmoid(a0 + za[:, W:])
    g = _bdot(_sigmoid(g_lo), gup_ref[...])
    project(0)
    fill(2)
    ret["scores"]()

    ones_g = (_blk((G, G), 0, RWKV_HEAD) == _blk((G, G), 1, RWKV_HEAD)).astype(BF16)

    def head_sum(x):
        return jnp.concatenate([_bdot(x[:, i * G:(i + 1) * G], ones_g) for i in range(n_groups)], axis=1)

    kk = k * k_k
    kk = kk * lax.rsqrt(jnp.maximum(head_sum(kk * kk), 1e-24))
    project(1)
    fill(2)
    k = k * (1.0 + (a - 1.0) * k_a)
    a_s = -kk
    b_s = kk * a
    bonus = head_sum(r * k * r_k) * v
    project(2)
    fill(2)
    ret["kv"]()

    ri, ci = _iota((tc, tc), 0), _iota((tc, tc), 1)
    same = _blk((tc, tc), 0, T) == _blk((tc, tc), 1, T)
    L = _xdot_l((same & (ci <= ri)).astype(BF16), ld)
    project(3)
    fill(2)
    LT = jnp.concatenate([jnp.broadcast_to(L[(c + 1) * T - 1:(c + 1) * T], (T, W))
                          for c in range(n_chunks)], axis=0)
    e_neg = jnp.exp2(-L)
    e_end = jnp.exp2(LT - L)
    rt = r * jnp.exp2(L)
    at = a_s * jnp.exp2(L - ld)
    bt, kt = b_s * e_neg, k * e_neg
    bh, kh = b_s * e_end, k * e_end
    w_end = jnp.exp2(LT)

    row, col = _iota((T, G), 0), _off((T, G), 1, T)
    strict, incl = col < row, col <= row
    eye_cat = (col == row).astype(F32)
    bd_mask = _blk((G, G), 0, T) == _blk((G, G), 1, T)
    eye_g = _iota((G, G), 0) == _iota((G, G), 1)

    def bd(x):
        return jnp.where(bd_mask, _tile_rows(x.astype(BF16), G // T), jnp.zeros((), BF16))

    units = [(gi, c) for c in range(n_chunks) for gi in range(n_groups)]
    sl = {u: (slice(u[1] * T, (u[1] + 1) * T), slice(u[0] * G, (u[0] + 1) * G)) for u in units}
    lhs = {u: jnp.concatenate([at[sl[u]], rt[sl[u]]], axis=0) for u in units}
    a1 = {u: _bdot_nt(lhs[u], bd(bt[sl[u]])) for u in units}
    a2 = {u: _bdot_nt(lhs[u], bd(kt[sl[u]])) for u in units}
    a_ab = {u: jnp.where(strict, a1[u][:T], 0.0) for u in units}
    a_rb = {u: jnp.where(incl, a1[u][T:], 0.0) for u in units}
    a_ak = {u: jnp.where(strict, a2[u][:T], 0.0) for u in units}
    a_rk = {u: jnp.where(incl, a2[u][T:], 0.0) for u in units}
    acc = {u: eye_cat + a_ab[u] for u in units}
    pw = {u: _bdot(a_ab[u], bd(a_ab[u])) for u in units}
    av = {u: _bdot(a_ak[u], bd(v[sl[u]])) for u in units}
    for _ in range(int(math.log2(T)) - 2):
        both = {u: _bdot(jnp.concatenate([pw[u], acc[u]], axis=0), bd(pw[u])) for u in units}
        pw = {u: both[u][:T] for u in units}
        acc = {u: acc[u] + both[u][T:] for u in units}
    tinv = {u: acc[u] + _bdot(acc[u], bd(pw[u])) for u in units}
    pu = {u: _bdot(tinv[u], jnp.concatenate([bd(at[sl[u]]), bd(av[u])], axis=1)) for u in units}
    p_c = {u: pu[u][:, :G] for u in units}
    u0 = {u: pu[u][:, G:] for u in units}
    half = G // 2
    zero_pair = jnp.zeros((T, half), BF16)
    m_bd, n_bd = {}, {}
    for u in units:
        mn = []
        for pr in range(2):
            ps = slice(pr * half, (pr + 1) * half)
            lhs_p = jnp.concatenate([bh[sl[u]][:, ps], kh[sl[u]][:, ps]], axis=0).astype(BF16)
            rhs_p = jnp.concatenate(
                [jnp.concatenate([p_c[u][:, ps].astype(BF16), u0[u][:, ps].astype(BF16)], axis=1),
                 jnp.concatenate([zero_pair, v[sl[u]][:, ps].astype(BF16)], axis=1)], axis=0)
            mn.append(_bdot_tn(lhs_p, rhs_p))
        zeros_h = jnp.zeros((half, half), F32)
        m_full = jnp.concatenate([jnp.concatenate([mn[0][:, :half], zeros_h], axis=1),
                                  jnp.concatenate([zeros_h, mn[1][:, :half]], axis=1)], axis=0)
        n_full = jnp.concatenate([jnp.concatenate([mn[0][:, half:], zeros_h], axis=1),
                                  jnp.concatenate([zeros_h, mn[1][:, half:]], axis=1)], axis=0)
        m_bd[u] = (jnp.where(bd_mask, m_full, 0.0)
                   + jnp.where(eye_g, w_end[u[1] * T:u[1] * T + 1, sl[u][1]], 0.0))
        n_bd[u] = jnp.where(bd_mask, n_full, 0.0)
    q_c = {u: rt[sl[u]] + _bdot(a_rb[u], bd(p_c[u])) for u in units}
    y0 = {u: _bdot(jnp.concatenate([a_rb[u], a_rk[u]], axis=1),
                   jnp.concatenate([bd(u0[u]), bd(v[sl[u]])], axis=0)) for u in units}
    st = [st_scr[gi] for gi in range(n_groups)]
    blocks = list(range(4, n_proj))
    for u in units:
        gi, c = u
        y_scr[sl[u]] = _bdot(q_c[u], st[gi]) + y0[u]
        st[gi] = _bdot(m_bd[u], st[gi]) + n_bd[u]
        if gi == n_groups - 1 and c + 1 < n_chunks:
            for _ in range(-(-len(blocks) // (n_chunks - 1 - c))):
                project(blocks.pop(0))
            fill(1)
    for gi in range(n_groups):
        st_scr[gi] = st[gi]
    ret["out"]()
    fill(8)

    y = y_scr[...]
    mean = head_sum(y) * (1.0 / RWKV_HEAD)
    yc = y - mean
    var = head_sum(yc * yc) * (1.0 / RWKV_HEAD)
    y = yc * lax.rsqrt(var + RWKV_GN_EPS) * ln_w + ln_b
    o_ref[...] = ((y + bonus) * g).astype(o_ref.dtype)
    ret["post"]()


def _layer0(x2d, g, w, cos, sin, mu, wa, gup, vec, tc, nt):
    scratch = [pltpu.VMEM((8, RWKV_COLS), F32),
               pltpu.VMEM((RWKV_WIDTH // MXU_TILE, MXU_TILE, MXU_TILE), F32),
               pltpu.VMEM((tc, RWKV_WIDTH), F32),
               pltpu.VMEM((RET_WIDTH // MXU_TILE, MXU_TILE, MXU_TILE), F32),
               pltpu.VMEM((tc, RET_WIDTH), F32)]
    return _mixer_call(functools.partial(_layer0_kernel, tc=tc, nt=nt), x2d, g, w, [mu, wa, gup, vec],
                       [RWKV_WIDTH, RET_WIDTH], tc, nt, scratch, "rwkv7_retention", time_tables=[cos, sin])


def _retention_parts(p, cos_ref, sin_ref, o_ref, st_scr, o_scr, tc):
    L = CHUNK
    G = MXU_TILE
    W = RET_WIDTH
    n_groups = W // G
    hpg = G // RET_HEAD
    cos, sin = cos_ref[...], sin_ref[...]

    def rope(x):
        return jnp.concatenate(
            [x[:, h * RET_HEAD:(h + 1) * RET_HEAD] * cos
             + pltpu.roll(x[:, h * RET_HEAD:(h + 1) * RET_HEAD], RET_HEAD // 2, 1) * sin
             for h in range(RET_HEADS)], axis=1)

    def lane_gamma(gi):
        lg = [math.log1p(-2.0 ** (-5.0 - (gi * hpg + h))) for h in range(hpg)]
        out = jnp.full((1, G), lg[-1], F32)
        for h in range(hpg - 1, -1, -1):
            out = jnp.where(_iota((1, G), 1) < (h + 1) * RET_HEAD, lg[h], out)
        return out

    rel = (_iota((L, G), 0) - _off((L, G), 1, L)).astype(F32)
    pos = _iota((L, G), 0).astype(F32)
    bd_mask = _blk((G, G), 0, RET_HEAD) == _blk((G, G), 1, RET_HEAD)

    def bd(x):
        return jnp.where(bd_mask, _tile_rows(x.astype(BF16), hpg), jnp.zeros((), BF16))

    units = [(gi, c) for c in range(tc // L) for gi in range(n_groups)]
    sl = {u: (slice(u[1] * L, (u[1] + 1) * L), slice(u[0] * G, (u[0] + 1) * G)) for u in units}
    lg = {gi: lane_gamma(gi) for gi in range(n_groups)}
    val = {}

    def scores():
        q = rope(p[:, 0:W])
        k = rope(p[:, W:2 * W]) * (RET_HEAD ** -0.5)
        val.update(q=q, k=k, v=p[:, 2 * W:3 * W])
        decay = {gi: jnp.where(rel >= 0, jnp.exp(lg[gi] * jnp.maximum(rel, 0.0)), 0.0) for gi in range(n_groups)}
        val["s"] = {u: _bdot_nt(q[sl[u]], bd(k[sl[u]])) * decay[u[0]] for u in units}

    def kv():
        k, v = val["k"], val["v"]
        k_dec = {gi: jnp.exp(lg[gi] * (L - 1.0 - pos)) for gi in range(n_groups)}
        val["kv"] = {u: jnp.where(bd_mask, _bdot_tn(k[sl[u]] * k_dec[u[0]], v[sl[u]]), 0.0) for u in units}

    def out():
        q, v = val["q"], val["v"]
        st = [st_scr[gi] for gi in range(n_groups)]
        for u in units:
            gi = u[0]
            q_dec = jnp.exp(lg[gi] * (pos + 1.0))
            c_dec = jnp.exp(lg[gi] * float(L))
            c_rows = jnp.concatenate([jnp.broadcast_to(c_dec[:, h * RET_HEAD:h * RET_HEAD + 1], (RET_HEAD, 1))
                                      for h in range(hpg)], axis=0)
            o_scr[sl[u]] = _bdot(val["s"][u], bd(v[sl[u]])) + _bdot(q[sl[u]], st[gi]) * q_dec
            st[gi] = st[gi] * c_rows + val["kv"][u]
        for gi in range(n_groups):
            st_scr[gi] = st[gi]

    def post():
        o = o_scr[...]
        gate = p[:, 3 * W:]
        outs = []
        for h in range(RET_HEADS):
            oh = o[:, h * RET_HEAD:(h + 1) * RET_HEAD]
            outs.append(oh * lax.rsqrt(jnp.mean(oh * oh, axis=-1, keepdims=True) + NORM_EPS))
        o_ref[...] = (jnp.concatenate(outs, axis=1) * (gate * _sigmoid(gate))).astype(o_ref.dtype)

    return dict(scores=scores, kv=kv, out=out, post=post)


def _mlstm_kernel(x_ref, g_ref, w_ref, cw_ref, vec_ref, gb_ref, o_ref,
                  p_scr, prev_scr, c_scr, m_scr, h_scr, *, tc, nt):
    L = CHUNK
    G = MXU_TILE
    hpg = G // ML_V
    GK = hpg * ML_QK
    n_groups = ML_HEADS // hpg
    n_chunks = tc // L
    QW = ML_HEADS * ML_QK

    @pl.when(_first_tile_of_row(nt))
    def _():
        prev_scr[...] = jnp.zeros_like(prev_scr)
        c_scr[...] = jnp.zeros_like(c_scr)
        m_scr[...] = jnp.zeros_like(m_scr)

    p, project = _project_tile(x_ref, g_ref, w_ref, p_scr, 5)
    D = D_MODEL
    conv_b, norm_w = vec_ref[0:1], vec_ref[1:2]
    x = p[:, :D]
    prev8 = prev_scr[...]
    prev_scr[...] = x[tc - 8:tc]
    acc = x * cw_ref[ML_CONV - 1:ML_CONV] + conv_b
    for j in range(ML_CONV - 1):
        acc = acc + _shift_rows(x, prev8, ML_CONV - 1 - j) * cw_ref[j:j + 1]
    qk = acc * _sigmoid(acc)
    q, k = qk[:, :QW], qk[:, QW:] * (ML_QK ** -0.5)
    v = p[:, D:2 * D]
    project(0)

    pre = p[:, 3 * D:] + gb_ref[...]
    cap = GATE_SOFTCAP * jnp.tanh(pre * (1.0 / GATE_SOFTCAP))
    li_all = cap
    lf_all = pltpu.roll(jnp.minimum(cap, 0.0) - jnp.log(1.0 + jnp.exp(-jnp.abs(cap))),
                        GATE_PAD - ML_HEADS, 1)

    tri = (_iota((L, L), 1) <= _iota((L, L), 0)).astype(BF16)
    causal = _iota((L, L), 1) <= _iota((L, L), 0)
    rows = _iota((L, 1), 0)
    k_mask = _blk((G, GK), 0, L) == _blk((G, GK), 1, ML_QK)
    v_mask = _blk((G, G), 0, L) == _blk((G, G), 1, ML_V)
    c_mask = _blk((GK, G), 0, ML_QK) == _blk((GK, G), 1, ML_V)
    c_mask2 = jnp.concatenate([c_mask, c_mask], axis=1)
    ones_bd = v_mask.astype(BF16)
    lane_k = _iota((1, GK), 1)
    row_k = _iota((GK, 1), 0)

    units = [(c, gi) for c in range(n_chunks) for gi in range(n_groups)]
    rs = {c: slice(c * L, (c + 1) * L) for c in range(n_chunks)}
    kl = {gi: slice(gi * GK, (gi + 1) * GK) for gi in range(n_groups)}
    vl = {gi: slice(gi * G, (gi + 1) * G) for gi in range(n_groups)}

    s_raw = {}
    v_bd = {}
    for (c, gi) in units:
        k_bd = jnp.where(k_mask, _tile_rows(k[rs[c], kl[gi]].astype(BF16), hpg), jnp.zeros((), BF16))
        v_bd[c, gi] = jnp.concatenate(
            [jnp.where(v_mask, _tile_rows(v[rs[c], vl[gi]].astype(BF16), hpg), jnp.zeros((), BF16)), ones_bd],
            axis=1)
        s_raw[c, gi] = _bdot_nt(q[rs[c], kl[gi]], k_bd)

    project(1)

    m_row = m_scr[0:1, :]
    gate = {}
    for c in range(n_chunks):
        li = li_all[rs[c]]
        b = _xdot_l(tri, lf_all[rs[c]], 3)
        beta = li - b
        cm = beta
        sh = 1
        while sh < L:
            cm = jnp.maximum(cm, jnp.where(rows >= sh, pltpu.roll(cm, sh, 0), -jnp.inf))
            sh *= 2
        mu = jnp.maximum(cm, m_row)
        mu_end = mu[L - 1:L]
        gate[c] = dict(beta_t=(beta * LOG2E).T, mu2=mu * LOG2E, inter=jnp.exp(m_row - mu),
                       emt=jnp.exp(-(b + mu)), kwf=jnp.exp(beta - mu_end))
        m_row = b[L - 1:L] + mu_end
    m_scr[0:1, :] = m_row
    project(2)

    s_dec, num, kv = {}, {}, {}
    for (c, gi) in units:
        gt = gate[c]
        dms = []
        for h in range(hpg):
            j = gi * hpg + h
            dms.append(jnp.where(causal, jnp.exp2(gt["beta_t"][j:j + 1, :] - gt["mu2"][:, j:j + 1]), 0.0))
        s_dec[c, gi] = s_raw[c, gi] * jnp.concatenate(dms, axis=1)
        kwf = gt["kwf"][:, gi * hpg + hpg - 1:gi * hpg + hpg]
        for h in range(hpg - 2, -1, -1):
            kwf = jnp.where(lane_k < (h + 1) * ML_QK, gt["kwf"][:, gi * hpg + h:gi * hpg + h + 1], kwf)
        kw = k[rs[c], kl[gi]] * kwf
        num[c, gi] = _bdot(s_dec[c, gi], v_bd[c, gi])
        v_one = jnp.concatenate([v[rs[c], vl[gi]].astype(BF16), jnp.ones((L, G), BF16)], axis=1)
        kv[c, gi] = jnp.where(c_mask2, _bdot_tn(kw, v_one), 0.0)

    project(3)

    c_st = [c_scr[gi] for gi in range(n_groups)]
    for (c, gi) in units:
        gt = gate[c]
        q_g = q[rs[c], kl[gi]]
        qc = _bdot(q_g, c_st[gi])
        cs = [gt["inter"][L - 1:L, gi * hpg + h:gi * hpg + h + 1] for h in range(hpg)]
        inter = jnp.concatenate([jnp.broadcast_to(gt["inter"][:, gi * hpg + h:gi * hpg + h + 1], (L, ML_V))
                                 for h in range(hpg)], axis=1)
        emt = jnp.concatenate([jnp.broadcast_to(gt["emt"][:, gi * hpg + h:gi * hpg + h + 1], (L, ML_V))
                               for h in range(hpg)], axis=1)
        den = num[c, gi][:, G:] + inter * qc[:, G:]
        rden = 1.0 / jnp.maximum(jnp.abs(den), emt)
        h_scr[rs[c], vl[gi]] = (num[c, gi][:, :G] + inter * qc[:, :G]) * rden
        cs_rows = cs[hpg - 1]
        for h in range(hpg - 2, -1, -1):
            cs_rows = jnp.where(row_k < (h + 1) * ML_QK, cs[h], cs_rows)
        c_st[gi] = cs_rows * c_st[gi] + kv[c, gi]
    for gi in range(n_groups):
        c_scr[gi] = c_st[gi]
    project(4)

    hfull = h_scr[...]
    outs = []
    for h in range(ML_HEADS):
        hh = hfull[:, h * ML_V:(h + 1) * ML_V]
        outs.append(hh * lax.rsqrt(jnp.mean(hh * hh, axis=-1, keepdims=True) + NORM_EPS))
    hn = jnp.concatenate(outs, axis=1) * norm_w
    o_ref[...] = (hn * _sigmoid(p[:, 2 * D:3 * D])).astype(o_ref.dtype)


def _mlstm(x2d, g, w, conv_w, vec, gbias, tc, nt):
    hpg = MXU_TILE // ML_V
    scratch = [pltpu.VMEM((8, D_MODEL), F32),
               pltpu.VMEM((ML_HEADS // hpg, hpg * ML_QK, 2 * MXU_TILE), F32),
               pltpu.VMEM((8, LANES), F32),
               pltpu.VMEM((tc, D_MODEL), F32)]
    return _mixer_call(functools.partial(_mlstm_kernel, tc=tc, nt=nt), x2d, g, w, [conv_w, vec, gbias],
                       [D_MODEL], tc, nt, scratch, "mlstm")[0]


def kernel(x, norm_mix_g, norm_mlp_g, norm_final_g, ab_w_in, rwkv_mu, rwkv_w0, rwkv_w_up, rwkv_a0,
           rwkv_a_up, rwkv_g_up, rwkv_k_k, rwkv_k_a, rwkv_r_k, rwkv_ln_w, rwkv_ln_b, ab_w_out,
           c_w_in, c_conv_w, c_conv_b, c_i_bias, c_f_bias, c_norm_w, c_w_out, mlp_w1, mlp_w2):
    bsz, seq, d = x.shape
    n = bsz * seq
    tm = MLP_TILE
    bf = lambda w: w.astype(BF16)
    x2d = x.reshape(n, d)

    w_in = ab_w_in[0]
    half = RWKV_LORA // 2
    zeros = jnp.zeros((half, RWKV_WIDTH), F32)
    wa = bf(jnp.concatenate([jnp.concatenate([rwkv_w_up[0], zeros], axis=1),
                             jnp.concatenate([zeros, rwkv_a_up[0]], axis=1)], axis=0))
    vec = jnp.stack([rwkv_w0[0], rwkv_a0[0], rwkv_k_k[0], rwkv_k_a[0], rwkv_r_k[0].reshape(-1),
                     rwkv_ln_w[0], rwkv_ln_b[0], jnp.zeros((RWKV_WIDTH,), F32)])
    pos = jnp.arange(seq, dtype=F32)
    inv = ROPE_BASE ** (-jnp.arange(0, RET_HEAD, 2, dtype=F32) / RET_HEAD)
    ang = pos[:, None] * inv[None, :]
    cos, sin = jnp.cos(ang), jnp.sin(ang)
    y_a, y_b = _layer0(x2d, norm_mix_g[0], bf(w_in), jnp.concatenate([cos, cos], axis=1),
                       jnp.concatenate([-sin, sin], axis=1), rwkv_mu[0].reshape(1, RWKV_COLS), wa,
                       bf(rwkv_g_up[0]), vec, RWKV_TILE, seq // RWKV_TILE)

    w_out = bf(ab_w_out[0])
    x2d = _mix_mlp([y_a, y_b], x2d, norm_mlp_g[0],
                   norm_final_g, [w_out[:RWKV_WIDTH], w_out[RWKV_WIDTH:]], bf(mlp_w1[0]), bf(mlp_w2[0]),
                   tm, final=False)

    w_in = jnp.pad(c_w_in[0], ((0, 0), (0, GATE_PAD - 2 * ML_HEADS)))
    gbias = jnp.pad(jnp.concatenate([c_i_bias[0], c_f_bias[0]]), (0, GATE_PAD - 2 * ML_HEADS)).reshape(1, GATE_PAD)
    vec = jnp.stack([c_conv_b[0], c_norm_w[0]] + [jnp.zeros((d,), F32)] * 6)
    y_c = _mlstm(x2d, norm_mix_g[1], bf(w_in), c_conv_w[0], vec, gbias, ML_TILE, seq // ML_TILE)

    x2d = _mix_mlp([y_c], x2d, norm_mlp_g[1], norm_final_g, [bf(c_w_out[0])],
                   bf(mlp_w1[1]), bf(mlp_w2[1]), tm, final=True)
    return x2d.reshape(bsz, seq, d)
```

```python
import functools
import math

import jax
import jax.numpy as jnp
from jax import lax
from jax.experimental import pallas as pl
from jax.experimental.pallas import tpu as pltpu

F32 = jnp.float32
BF16 = jnp.bfloat16

D_MODEL = 1024
D_FF = 4 * D_MODEL
NORM_EPS = 1e-6

RWKV_WIDTH = 512
RWKV_HEAD = 64
RWKV_LORA = 128
RWKV_GATE_LORA = 128
RWKV_COLS = 3 * RWKV_WIDTH + RWKV_LORA + RWKV_GATE_LORA
RWKV_GN_EPS = 64e-5
RWKV_CHUNK = 64

RET_WIDTH = 512
RET_HEADS = 4
RET_HEAD = 128
RET_COLS = 4 * RET_WIDTH
ROPE_BASE = 10000.0
CHUNK = 128

ML_HEADS = 8
ML_QK = 64
ML_V = 128
ML_CONV = 4
GATE_SOFTCAP = 15.0
LOG2E = math.log2(math.e)
GATE_PAD = 128

MLP_TILE = 512
RWKV_TILE = 512
ML_TILE = 512

LANES = 128
MXU_TILE = 256
VMEM_LIMIT = 56 * 1024 * 1024


def _iota(shape, dim):
    return lax.broadcasted_iota(jnp.int32, shape, dim)


def _blk(shape, dim, n):
    return lax.shift_right_logical(_iota(shape, dim), int(math.log2(n)))


def _off(shape, dim, n):
    return _iota(shape, dim) & (n - 1)


def _bdot(a, b):
    return jnp.dot(a.astype(BF16), b.astype(BF16), preferred_element_type=F32)


def _bdot_nt(a, b):
    return lax.dot_general(a.astype(BF16), b.astype(BF16), (((1,), (1,)), ((), ())),
                           preferred_element_type=F32)


def _bdot_tn(a, b):
    return lax.dot_general(a.astype(BF16), b.astype(BF16), (((0,), (0,)), ((), ())),
                           preferred_element_type=F32)


def _split(x, n):
    parts = []
    rem = x
    for _ in range(n):
        p = rem.astype(BF16)
        parts.append(p)
        rem = rem - p.astype(F32)
    return parts


def _xdot_l(m01, x, n=2):
    return sum(jnp.dot(m01, p, preferred_element_type=F32) for p in _split(x, n))


def _xdot_r(x, m01, n=2):
    return sum(jnp.dot(p, m01, preferred_element_type=F32) for p in _split(x, n))


def _sigmoid(x):
    return 1.0 / (1.0 + jnp.exp(-x))


def _rms(x, g):
    return x * lax.rsqrt(jnp.mean(x * x, axis=-1, keepdims=True) + NORM_EPS) * g


def _tile_rows(x, reps):
    return jnp.concatenate([x] * reps, axis=0)


def _shift_rows(x, prev8, k):
    rolled = pltpu.roll(x, k, 0)
    head = jnp.where(_iota((8, 1), 0) < k, pltpu.roll(prev8, k, 0), rolled[0:8])
    return jnp.concatenate([head, rolled[8:]], axis=0)


def _const_spec(shape):
    return pl.BlockSpec(shape, lambda *_: (0,) * len(shape))


def _params(n_grid):
    return pltpu.CompilerParams(dimension_semantics=("arbitrary",) * n_grid,
                                vmem_limit_bytes=VMEM_LIMIT)


def _project_tile(x_ref, g_ref, w_ref, p_scr):
    @pl.when(pl.program_id(0) == 0)
    def _():
        p_scr[...] = jnp.zeros_like(p_scr)

    p_prev = p_scr[...]
    h = _rms(x_ref[...], g_ref[...]).astype(BF16)

    def emit(i):
        cols = slice(i * MXU_TILE, min((i + 1) * MXU_TILE, w_ref.shape[1]))
        p_scr[:, cols] = jnp.dot(h, w_ref[:, cols], preferred_element_type=F32)

    return p_prev, emit


def _first_tile_of_row(nt):
    return lax.rem(jnp.maximum(pl.program_id(0) - 1, 0), nt) == 0


def _mixer_call(kernel_fn, x2d, g, w, consts, out_widths, tc, nt, scratch, name, time_tables=()):
    n = x2d.shape[0]
    total = n // tc
    once = pl.Buffered(1)
    in_specs = [pl.BlockSpec((tc, D_MODEL), lambda s: (jnp.minimum(s, total - 1), 0)),
                _const_spec((1, D_MODEL)),
                pl.BlockSpec(w.shape, lambda s: (0, 0), pipeline_mode=once)]
    in_specs += [pl.BlockSpec((tc, t.shape[1]), lambda s: (lax.rem(jnp.maximum(s - 1, 0), nt), 0))
                 for t in time_tables]
    in_specs += [_const_spec(c.shape) for c in consts]
    return pl.pallas_call(
        kernel_fn, grid=(total + 1,), in_specs=in_specs,
        out_specs=[pl.BlockSpec((tc, ow), lambda s: (jnp.maximum(s - 1, 0), 0)) for ow in out_widths],
        out_shape=[jax.ShapeDtypeStruct((n, ow), BF16) for ow in out_widths],
        scratch_shapes=[pltpu.VMEM((tc, w.shape[1]), F32)] + scratch,
        compiler_params=_params(1), name=name,
    )(x2d, g.reshape(1, D_MODEL), w, *time_tables, *consts)


def _mix_mlp_kernel(*refs, n_y, final):
    y_refs, rest = refs[:n_y], refs[n_y:]
    x_ref, g_ref, gf_ref = rest[0], rest[1], rest[2]
    wo_refs, (w1_ref, w2_ref, o_ref) = rest[3:3 + n_y], rest[3 + n_y:]
    x1 = x_ref[...]
    for y_ref, wo_ref in zip(y_refs, wo_refs):
        x1 = x1 + jnp.dot(y_ref[...], wo_ref[...], preferred_element_type=F32)
    h = _rms(x1, g_ref[...]).astype(BF16)
    u = jnp.dot(h, w1_ref[...], preferred_element_type=F32)
    u = jnp.square(jnp.maximum(u, 0.0)).astype(BF16)
    x2 = x1 + jnp.dot(u, w2_ref[...], preferred_element_type=F32)
    if final:
        x2 = _rms(x2, gf_ref[...])
    o_ref[...] = x2


def _mix_mlp(ys, x2d, g, gf, w_outs, w1, w2, tm, final):
    n = x2d.shape[0]
    once = pl.Buffered(1)
    in_specs = [pl.BlockSpec((tm, y.shape[1]), lambda i: (i, 0)) for y in ys]
    in_specs += [pl.BlockSpec((tm, D_MODEL), lambda i: (i, 0)),
                 _const_spec((1, D_MODEL)), _const_spec((1, D_MODEL))]
    in_specs += [pl.BlockSpec(w.shape, lambda i: (0, 0), pipeline_mode=once) for w in w_outs]
    in_specs += [pl.BlockSpec(w1.shape, lambda i: (0, 0), pipeline_mode=once),
                 pl.BlockSpec(w2.shape, lambda i: (0, 0), pipeline_mode=once)]
    return pl.pallas_call(
        functools.partial(_mix_mlp_kernel, n_y=len(ys), final=final),
        grid=(n // tm,), in_specs=in_specs,
        out_specs=pl.BlockSpec((tm, D_MODEL), lambda i: (i, 0)),
        out_shape=jax.ShapeDtypeStruct((n, D_MODEL), F32),
        compiler_params=_params(1), name="mix_mlp",
    )(*ys, x2d, g.reshape(1, D_MODEL), gf.reshape(1, D_MODEL), *w_outs, w1, w2)


def _layer0_kernel(x_ref, g_ref, w_ref, cos_ref, sin_ref, mu_ref, wa_ref, gup_ref, vec_ref, o_ref, ob_ref,
                   p_scr, prev_scr, st_scr, y_scr, rst_scr, ro_scr, *, tc, nt):
    T = RWKV_CHUNK
    G = MXU_TILE
    n_groups = RWKV_WIDTH // G
    n_chunks = tc // T

    @pl.when(_first_tile_of_row(nt))
    def _():
        prev_scr[...] = jnp.zeros_like(prev_scr)
        st_scr[...] = jnp.zeros_like(st_scr)
        rst_scr[...] = jnp.zeros_like(rst_scr)

    n_proj = RWKV_COLS // MXU_TILE
    p_all, project = _project_tile(x_ref, g_ref, w_ref, p_scr)
    p_raw = p_all[:, :RWKV_COLS]
    ret = _retention_parts(p_all[:, RWKV_COLS:], cos_ref, sin_ref, ob_ref, rst_scr, ro_scr, tc)
    ret_blocks = list(range(n_proj, (RWKV_COLS + RET_COLS) // MXU_TILE))

    def fill(n):
        for _ in range(min(n, len(ret_blocks))):
            project(ret_blocks.pop(0))
    shifted = _shift_rows(p_raw, prev_scr[...], 1)
    prev_scr[...] = p_raw[tc - 8:tc]
    p = p_raw + (shifted - p_raw) * mu_ref[...]

    w0, a0, k_k, k_a = (vec_ref[i:i + 1] for i in range(4))
    r_k, ln_w, ln_b = (vec_ref[i:i + 1] for i in range(4, 7))
    W = RWKV_WIDTH
    r, k, v = p[:, 0:W], p[:, W:2 * W], p[:, 2 * W:3 * W]
    lora = p[:, 3 * W:3 * W + RWKV_LORA]
    g_lo = p[:, 3 * W + RWKV_LORA:]

    lora = jnp.where(_iota((1, RWKV_LORA), 1) < RWKV_LORA // 2, jnp.tanh(lora), lora)
    za = _bdot(lora, wa_ref[...])
    ld = (-math.exp(-0.5) * LOG2E) * _sigmoid(w0 + za[:, :W])
    a = _sigmoid(a0 + za[:, W:])
    g = _bdot(_sigmoid(g_lo), gup_ref[...])
    project(0)
    fill(2)
    ret["scores"]()

    ones_g = (_blk((G, G), 0, RWKV_HEAD) == _blk((G, G), 1, RWKV_HEAD)).astype(BF16)

    def head_sum(x):
        return jnp.concatenate([_bdot(x[:, i * G:(i + 1) * G], ones_g) for i in range(n_groups)], axis=1)

    kk = k * k_k
    kk = kk * lax.rsqrt(jnp.maximum(head_sum(kk * kk), 1e-24))
    project(1)
    fill(2)
    k = k * (1.0 + (a - 1.0) * k_a)
    a_s = -kk
    b_s = kk * a
    bonus = head_sum(r * k * r_k) * v
    project(2)
    fill(2)
    ret["kv"]()

    ri, ci = _iota((tc, tc), 0), _iota((tc, tc), 1)
    same = _blk((tc, tc), 0, T) == _blk((tc, tc), 1, T)
    L = _xdot_l((same & (ci <= ri)).astype(BF16), ld)
    project(3)
    fill(2)
    LT = jnp.concatenate([jnp.broadcast_to(L[(c + 1) * T - 1:(c + 1) * T], (T, W))
                          for c in range(n_chunks)], axis=0)
    e_neg = jnp.exp2(-L)
    e_end = jnp.exp2(LT - L)
    rt = r * jnp.exp2(L)
    at = a_s * jnp.exp2(L - ld)
    bt, kt = b_s * e_neg, k * e_neg
    bh, kh = b_s * e_end, k * e_end
    w_end = jnp.exp2(LT)

    row, col = _iota((T, G), 0), _off((T, G), 1, T)
    strict, incl = col < row, col <= row
    eye_cat = (col == row).astype(F32)
    bd_mask = _blk((G, G), 0, T) == _blk((G, G), 1, T)
    eye_g = _iota((G, G), 0) == _iota((G, G), 1)

    def bd(x):
        return jnp.where(bd_mask, _tile_rows(x.astype(BF16), G // T), jnp.zeros((), BF16))

    units = [(gi, c) for c in range(n_chunks) for gi in range(n_groups)]
    sl = {u: (slice(u[1] * T, (u[1] + 1) * T), slice(u[0] * G, (u[0] + 1) * G)) for u in units}
    lhs = {u: jnp.concatenate([at[sl[u]], rt[sl[u]]], axis=0) for u in units}
    a1 = {u: _bdot_nt(lhs[u], bd(bt[sl[u]])) for u in units}
    a2 = {u: _bdot_nt(lhs[u], bd(kt[sl[u]])) for u in units}
    a_ab = {u: jnp.where(strict, a1[u][:T], 0.0) for u in units}
    a_rb = {u: jnp.where(incl, a1[u][T:], 0.0) for u in units}
    a_ak = {u: jnp.where(strict, a2[u][:T], 0.0) for u in units}
    a_rk = {u: jnp.where(incl, a2[u][T:], 0.0) for u in units}
    acc = {u: eye_cat + a_ab[u] for u in units}
    pw = {u: _bdot(a_ab[u], bd(a_ab[u])) for u in units}
    av = {u: _bdot(a_ak[u], bd(v[sl[u]])) for u in units}
    for _ in range(int(math.log2(T)) - 2):
        both = {u: _bdot(jnp.concatenate([pw[u], acc[u]], axis=0), bd(pw[u])) for u in units}
        pw = {u: both[u][:T] for u in units}
        acc = {u: acc[u] + both[u][T:] for u in units}
    tinv = {u: acc[u] + _bdot(acc[u], bd(pw[u])) for u in units}
    pu = {u: _bdot(tinv[u], jnp.concatenate([bd(at[sl[u]]), bd(av[u])], axis=1)) for u in units}
    p_c = {u: pu[u][:, :G] for u in units}
    u0 = {u: pu[u][:, G:] for u in units}
    half = G // 2
    zero_pair = jnp.zeros((T, half), BF16)
    m_bd, n_bd = {}, {}
    for u in units:
        mn = []
        for pr in range(2):
            ps = slice(pr * half, (pr + 1) * half)
            lhs_p = jnp.concatenate([bh[sl[u]][:, ps], kh[sl[u]][:, ps]], axis=0).astype(BF16)
            rhs_p = jnp.concatenate(
                [jnp.concatenate([p_c[u][:, ps].astype(BF16), u0[u][:, ps].astype(BF16)], axis=1),
                 jnp.concatenate([zero_pair, v[sl[u]][:, ps].astype(BF16)], axis=1)], axis=0)
            mn.append(_bdot_tn(lhs_p, rhs_p))
        zeros_h = jnp.zeros((half, half), F32)
        m_full = jnp.concatenate([jnp.concatenate([mn[0][:, :half], zeros_h], axis=1),
                                  jnp.concatenate([zeros_h, mn[1][:, :half]], axis=1)], axis=0)
        n_full = jnp.concatenate([jnp.concatenate([mn[0][:, half:], zeros_h], axis=1),
                                  jnp.concatenate([zeros_h, mn[1][:, half:]], axis=1)], axis=0)
        m_bd[u] = (jnp.where(bd_mask, m_full, 0.0)
                   + jnp.where(eye_g, w_end[u[1] * T:u[1] * T + 1, sl[u][1]], 0.0))
        n_bd[u] = jnp.where(bd_mask, n_full, 0.0)
    q_c = {u: rt[sl[u]] + _bdot(a_rb[u], bd(p_c[u])) for u in units}
    y0 = {u: _bdot(jnp.concatenate([a_rb[u], a_rk[u]], axis=1),
                   jnp.concatenate([bd(u0[u]), bd(v[sl[u]])], axis=0)) for u in units}
    st = [st_scr[gi] for gi in range(n_groups)]
    blocks = list(range(4, n_proj))
    for u in units:
        gi, c = u
        y_scr[sl[u]] = _bdot(q_c[u], st[gi]) + y0[u]
        st[gi] = _bdot(m_bd[u], st[gi]) + n_bd[u]
        if gi == n_groups - 1 and c + 1 < n_chunks:
            for _ in range(-(-len(blocks) // (n_chunks - 1 - c))):
                project(blocks.pop(0))
            fill(1)
    for gi in range(n_groups):
        st_scr[gi] = st[gi]
    ret["out"]()
    fill(8)

    y = y_scr[...]
    mean = head_sum(y) * (1.0 / RWKV_HEAD)
    yc = y - mean
    var = head_sum(yc * yc) * (1.0 / RWKV_HEAD)
    y = yc * lax.rsqrt(var + RWKV_GN_EPS) * ln_w + ln_b
    o_ref[...] = ((y + bonus) * g).astype(o_ref.dtype)
    ret["post"]()


def _layer0(x2d, g, w, cos, sin, mu, wa, gup, vec, tc, nt):
    scratch = [pltpu.VMEM((8, RWKV_COLS), F32),
               pltpu.VMEM((RWKV_WIDTH // MXU_TILE, MXU_TILE, MXU_TILE), F32),
               pltpu.VMEM((tc, RWKV_WIDTH), F32),
               pltpu.VMEM((RET_WIDTH // MXU_TILE, MXU_TILE, MXU_TILE), F32),
               pltpu.VMEM((tc, RET_WIDTH), F32)]
    return _mixer_call(functools.partial(_layer0_kernel, tc=tc, nt=nt), x2d, g, w, [mu, wa, gup, vec],
                       [RWKV_WIDTH, RET_WIDTH], tc, nt, scratch, "rwkv7_retention", time_tables=[cos, sin])


def _retention_parts(p, cos_ref, sin_ref, o_ref, st_scr, o_scr, tc):
    L = CHUNK
    G = MXU_TILE
    W = RET_WIDTH
    n_groups = W // G
    hpg = G // RET_HEAD
    cos, sin = cos_ref[...], sin_ref[...]

    def rope(x):
        return jnp.concatenate(
            [x[:, h * RET_HEAD:(h + 1) * RET_HEAD] * cos
             + pltpu.roll(x[:, h * RET_HEAD:(h + 1) * RET_HEAD], RET_HEAD // 2, 1) * sin
             for h in range(RET_HEADS)], axis=1)

    def lane_gamma(gi):
        lg = [math.log1p(-2.0 ** (-5.0 - (gi * hpg + h))) for h in range(hpg)]
        out = jnp.full((1, G), lg[-1], F32)
        for h in range(hpg - 1, -1, -1):
            out = jnp.where(_iota((1, G), 1) < (h + 1) * RET_HEAD, lg[h], out)
        return out

    rel = (_iota((L, G), 0) - _off((L, G), 1, L)).astype(F32)
    pos = _iota((L, G), 0).astype(F32)
    bd_mask = _blk((G, G), 0, RET_HEAD) == _blk((G, G), 1, RET_HEAD)

    def bd(x):
        return jnp.where(bd_mask, _tile_rows(x.astype(BF16), hpg), jnp.zeros((), BF16))

    units = [(gi, c) for c in range(tc // L) for gi in range(n_groups)]
    sl = {u: (slice(u[1] * L, (u[1] + 1) * L), slice(u[0] * G, (u[0] + 1) * G)) for u in units}
    lg = {gi: lane_gamma(gi) for gi in range(n_groups)}
    val = {}

    def scores():
        q = rope(p[:, 0:W])
        k = rope(p[:, W:2 * W]) * (RET_HEAD ** -0.5)
        val.update(q=q, k=k, v=p[:, 2 * W:3 * W])
        decay = {gi: jnp.where(rel >= 0, jnp.exp(lg[gi] * jnp.maximum(rel, 0.0)), 0.0) for gi in range(n_groups)}
        val["s"] = {u: _bdot_nt(q[sl[u]], bd(k[sl[u]])) * decay[u[0]] for u in units}

    def kv():
        k, v = val["k"], val["v"]
        k_dec = {gi: jnp.exp(lg[gi] * (L - 1.0 - pos)) for gi in range(n_groups)}
        val["kv"] = {u: jnp.where(bd_mask, _bdot_tn(k[sl[u]] * k_dec[u[0]], v[sl[u]]), 0.0) for u in units}

    def out():
        q, v = val["q"], val["v"]
        st = [st_scr[gi] for gi in range(n_groups)]
        for u in units:
            gi = u[0]
            q_dec = jnp.exp(lg[gi] * (pos + 1.0))
            c_dec = jnp.exp(lg[gi] * float(L))
            c_rows = jnp.concatenate([jnp.broadcast_to(c_dec[:, h * RET_HEAD:h * RET_HEAD + 1], (RET_HEAD, 1))
                                      for h in range(hpg)], axis=0)
            o_scr[sl[u]] = _bdot(val["s"][u], bd(v[sl[u]])) + _bdot(q[sl[u]], st[gi]) * q_dec
            st[gi] = st[gi] * c_rows + val["kv"][u]
        for gi in range(n_groups):
            st_scr[gi] = st[gi]

    def post():
        o = o_scr[...]
        gate = p[:, 3 * W:]
        outs = []
        for h in range(RET_HEADS):
            oh = o[:, h * RET_HEAD:(h + 1) * RET_HEAD]
            outs.append(oh * lax.rsqrt(jnp.mean(oh * oh, axis=-1, keepdims=True) + NORM_EPS))
        o_ref[...] = (jnp.concatenate(outs, axis=1) * (gate * _sigmoid(gate))).astype(o_ref.dtype)

    return dict(scores=scores, kv=kv, out=out, post=post)


def _mlstm_kernel(x_ref, g_ref, w_ref, cw_ref, vec_ref, gb_ref, o_ref,
                  p_scr, prev_scr, c_scr, m_scr, h_scr, *, tc, nt):
    L = CHUNK
    G = MXU_TILE
    hpg = G // ML_V
    GK = hpg * ML_QK
    n_groups = ML_HEADS // hpg
    n_chunks = tc // L
    QW = ML_HEADS * ML_QK

    @pl.when(_first_tile_of_row(nt))
    def _():
        prev_scr[...] = jnp.zeros_like(prev_scr)
        c_scr[...] = jnp.zeros_like(c_scr)
        m_scr[...] = jnp.zeros_like(m_scr)

    p, project = _project_tile(x_ref, g_ref, w_ref, p_scr)

    def project_blocks(*blocks):
        for i in blocks:
            project(i)
    D = D_MODEL
    conv_b, norm_w = vec_ref[0:1], vec_ref[1:2]
    x = p[:, :D]
    prev8 = prev_scr[...]
    prev_scr[...] = x[tc - 8:tc]
    acc = x * cw_ref[ML_CONV - 1:ML_CONV] + conv_b
    for j in range(ML_CONV - 1):
        acc = acc + _shift_rows(x, prev8, ML_CONV - 1 - j) * cw_ref[j:j + 1]
    qk = acc * _sigmoid(acc)
    q, k = qk[:, :QW], qk[:, QW:] * (ML_QK ** -0.5)
    v = p[:, D:2 * D]
    project_blocks(0, 1, 2)

    pre = p[:, 3 * D:] + gb_ref[...]
    cap = GATE_SOFTCAP * jnp.tanh(pre * (1.0 / GATE_SOFTCAP))
    li_all = cap
    lf_all = pltpu.roll(jnp.minimum(cap, 0.0) - jnp.log(1.0 + jnp.exp(-jnp.abs(cap))),
                        GATE_PAD - ML_HEADS, 1)

    tri = (_iota((L, L), 1) <= _iota((L, L), 0)).astype(BF16)
    causal = _iota((L, L), 1) <= _iota((L, L), 0)
    rows = _iota((L, 1), 0)
    k_mask = _blk((G, GK), 0, L) == _blk((G, GK), 1, ML_QK)
    v_mask = _blk((G, G), 0, L) == _blk((G, G), 1, ML_V)
    c_mask = _blk((GK, G), 0, ML_QK) == _blk((GK, G), 1, ML_V)
    c_mask2 = jnp.concatenate([c_mask, c_mask], axis=1)
    ones_bd = v_mask.astype(BF16)
    lane_k = _iota((1, GK), 1)
    row_k = _iota((GK, 1), 0)

    units = [(c, gi) for c in range(n_chunks) for gi in range(n_groups)]
    rs = {c: slice(c * L, (c + 1) * L) for c in range(n_chunks)}
    kl = {gi: slice(gi * GK, (gi + 1) * GK) for gi in range(n_groups)}
    vl = {gi: slice(gi * G, (gi + 1) * G) for gi in range(n_groups)}

    s_raw = {}
    v_bd = {}
    for (c, gi) in units:
        k_bd = jnp.where(k_mask, _tile_rows(k[rs[c], kl[gi]].astype(BF16), hpg), jnp.zeros((), BF16))
        v_bd[c, gi] = jnp.concatenate(
            [jnp.where(v_mask, _tile_rows(v[rs[c], vl[gi]].astype(BF16), hpg), jnp.zeros((), BF16)), ones_bd],
            axis=1)
        s_raw[c, gi] = _bdot_nt(q[rs[c], kl[gi]], k_bd)

    project_blocks(3, 4, 5)

    m_row = m_scr[0:1, :]
    gate = {}
    for c in range(n_chunks):
        li = li_all[rs[c]]
        b = _xdot_l(tri, lf_all[rs[c]], 3)
        beta = li - b
        cm = beta
        sh = 1
        while sh < L:
            cm = jnp.maximum(cm, jnp.where(rows >= sh, pltpu.roll(cm, sh, 0), -jnp.inf))
            sh *= 2
        mu = jnp.maximum(cm, m_row)
        mu_end = mu[L - 1:L]
        gate[c] = dict(beta_t=(beta * LOG2E).T, mu2=mu * LOG2E, inter=jnp.exp(m_row - mu),
                       emt=jnp.exp(-(b + mu)), kwf=jnp.exp(beta - mu_end))
        m_row = b[L - 1:L] + mu_end
    m_scr[0:1, :] = m_row
    project_blocks(6, 7, 8)

    s_dec, num, kv = {}, {}, {}
    for (c, gi) in units:
        gt = gate[c]
        dms = []
        for h in range(hpg):
            j = gi * hpg + h
            dms.append(jnp.where(causal, jnp.exp2(gt["beta_t"][j:j + 1, :] - gt["mu2"][:, j:j + 1]), 0.0))
        s_dec[c, gi] = s_raw[c, gi] * jnp.concatenate(dms, axis=1)
        kwf = gt["kwf"][:, gi * hpg + hpg - 1:gi * hpg + hpg]
        for h in range(hpg - 2, -1, -1):
            kwf = jnp.where(lane_k < (h + 1) * ML_QK, gt["kwf"][:, gi * hpg + h:gi * hpg + h + 1], kwf)
        kw = k[rs[c], kl[gi]] * kwf
        num[c, gi] = _bdot(s_dec[c, gi], v_bd[c, gi])
        v_one = jnp.concatenate([v[rs[c], vl[gi]].astype(BF16), jnp.ones((L, G), BF16)], axis=1)
        kv[c, gi] = jnp.where(c_mask2, _bdot_tn(kw, v_one), 0.0)

    project_blocks(9, 10)

    c_st = [c_scr[gi] for gi in range(n_groups)]
    for (c, gi) in units:
        gt = gate[c]
        q_g = q[rs[c], kl[gi]]
        qc = _bdot(q_g, c_st[gi])
        cs = [gt["inter"][L - 1:L, gi * hpg + h:gi * hpg + h + 1] for h in range(hpg)]
        inter = jnp.concatenate([jnp.broadcast_to(gt["inter"][:, gi * hpg + h:gi * hpg + h + 1], (L, ML_V))
                                 for h in range(hpg)], axis=1)
        emt = jnp.concatenate([jnp.broadcast_to(gt["emt"][:, gi * hpg + h:gi * hpg + h + 1], (L, ML_V))
                               for h in range(hpg)], axis=1)
        den = num[c, gi][:, G:] + inter * qc[:, G:]
        rden = 1.0 / jnp.maximum(jnp.abs(den), emt)
        h_scr[rs[c], vl[gi]] = (num[c, gi][:, :G] + inter * qc[:, :G]) * rden
        cs_rows = cs[hpg - 1]
        for h in range(hpg - 2, -1, -1):
            cs_rows = jnp.where(row_k < (h + 1) * ML_QK, cs[h], cs_rows)
        c_st[gi] = cs_rows * c_st[gi] + kv[c, gi]
    for gi in range(n_groups):
        c_scr[gi] = c_st[gi]
    project_blocks(11, 12)

    hfull = h_scr[...]
    outs = []
    for h in range(ML_HEADS):
        hh = hfull[:, h * ML_V:(h + 1) * ML_V]
        outs.append(hh * lax.rsqrt(jnp.mean(hh * hh, axis=-1, keepdims=True) + NORM_EPS))
    hn = jnp.concatenate(outs, axis=1) * norm_w
    o_ref[...] = (hn * _sigmoid(p[:, 2 * D:3 * D])).astype(o_ref.dtype)


def _mlstm(x2d, g, w, conv_w, vec, gbias, tc, nt):
    hpg = MXU_TILE // ML_V
    scratch = [pltpu.VMEM((8, D_MODEL), F32),
               pltpu.VMEM((ML_HEADS // hpg, hpg * ML_QK, 2 * MXU_TILE), F32),
               pltpu.VMEM((8, LANES), F32),
               pltpu.VMEM((tc, D_MODEL), F32)]
    return _mixer_call(functools.partial(_mlstm_kernel, tc=tc, nt=nt), x2d, g, w, [conv_w, vec, gbias],
                       [D_MODEL], tc, nt, scratch, "mlstm")[0]


def kernel(x, norm_mix_g, norm_mlp_g, norm_final_g, ab_w_in, rwkv_mu, rwkv_w0, rwkv_w_up, rwkv_a0,
           rwkv_a_up, rwkv_g_up, rwkv_k_k, rwkv_k_a, rwkv_r_k, rwkv_ln_w, rwkv_ln_b, ab_w_out,
           c_w_in, c_conv_w, c_conv_b, c_i_bias, c_f_bias, c_norm_w, c_w_out, mlp_w1, mlp_w2):
    bsz, seq, d = x.shape
    n = bsz * seq
    tm = MLP_TILE
    bf = lambda w: w.astype(BF16)
    x2d = x.reshape(n, d)

    w_in = ab_w_in[0]
    half = RWKV_LORA // 2
    zeros = jnp.zeros((half, RWKV_WIDTH), F32)
    wa = bf(jnp.concatenate([jnp.concatenate([rwkv_w_up[0], zeros], axis=1),
                             jnp.concatenate([zeros, rwkv_a_up[0]], axis=1)], axis=0))
    vec = jnp.stack([rwkv_w0[0], rwkv_a0[0], rwkv_k_k[0], rwkv_k_a[0], rwkv_r_k[0].reshape(-1),
                     rwkv_ln_w[0], rwkv_ln_b[0], jnp.zeros((RWKV_WIDTH,), F32)])
    pos = jnp.arange(seq, dtype=F32)
    inv = ROPE_BASE ** (-jnp.arange(0, RET_HEAD, 2, dtype=F32) / RET_HEAD)
    ang = pos[:, None] * inv[None, :]
    cos, sin = jnp.cos(ang), jnp.sin(ang)
    y_a, y_b = _layer0(x2d, norm_mix_g[0], bf(w_in), jnp.concatenate([cos, cos], axis=1),
                       jnp.concatenate([-sin, sin], axis=1), rwkv_mu[0].reshape(1, RWKV_COLS), wa,
                       bf(rwkv_g_up[0]), vec, RWKV_TILE, seq // RWKV_TILE)

    w_out = bf(ab_w_out[0])
    x2d = _mix_mlp([y_a, y_b], x2d, norm_mlp_g[0],
                   norm_final_g, [w_out[:RWKV_WIDTH], w_out[RWKV_WIDTH:]], bf(mlp_w1[0]), bf(mlp_w2[0]),
                   tm, final=False)

    w_in = jnp.pad(c_w_in[0], ((0, 0), (0, GATE_PAD - 2 * ML_HEADS)))
    gbias = jnp.pad(jnp.concatenate([c_i_bias[0], c_f_bias[0]]), (0, GATE_PAD - 2 * ML_HEADS)).reshape(1, GATE_PAD)
    vec = jnp.stack([c_conv_b[0], c_norm_w[0]] + [jnp.zeros((d,), F32)] * 6)
    y_c = _mlstm(x2d, norm_mix_g[1], bf(w_in), c_conv_w[0], vec, gbias, ML_TILE, seq // ML_TILE)

    x2d = _mix_mlp([y_c], x2d, norm_mlp_g[1], norm_final_g, [bf(c_w_out[0])],
                   bf(mlp_w1[1]), bf(mlp_w2[1]), tm, final=True)
    return x2d.reshape(bsz, seq, d)
```

```python
import functools
import math

import jax
import jax.numpy as jnp
from jax import lax
from jax.experimental import pallas as pl
from jax.experimental.pallas import tpu as pltpu

F32 = jnp.float32
BF16 = jnp.bfloat16

D_MODEL = 1024
D_FF = 4 * D_MODEL
NORM_EPS = 1e-6

RWKV_WIDTH = 512
RWKV_HEAD = 64
RWKV_LORA = 128
RWKV_GATE_LORA = 128
RWKV_COLS = 3 * RWKV_WIDTH + RWKV_LORA + RWKV_GATE_LORA
RWKV_GN_EPS = 64e-5
RWKV_CHUNK = 64

RET_WIDTH = 512
RET_HEADS = 4
RET_HEAD = 128
RET_COLS = 4 * RET_WIDTH
ROPE_BASE = 10000.0
CHUNK = 128

ML_HEADS = 8
ML_QK = 64
ML_V = 128
ML_CONV = 4
GATE_SOFTCAP = 15.0
LOG2E = math.log2(math.e)
GATE_PAD = 128

MLP_TILE = 512
RWKV_TILE = 512
ML_TILE = 512

LANES = 128
MXU_TILE = 256
VMEM_LIMIT = 56 * 1024 * 1024


def _iota(shape, dim):
    return lax.broadcasted_iota(jnp.int32, shape, dim)


def _blk(shape, dim, n):
    return lax.shift_right_logical(_iota(shape, dim), int(math.log2(n)))


def _off(shape, dim, n):
    return _iota(shape, dim) & (n - 1)


def _bdot(a, b):
    return jnp.dot(a.astype(BF16), b.astype(BF16), preferred_element_type=F32)


def _bdot_nt(a, b):
    return lax.dot_general(a.astype(BF16), b.astype(BF16), (((1,), (1,)), ((), ())),
                           preferred_element_type=F32)


def _bdot_tn(a, b):
    return lax.dot_general(a.astype(BF16), b.astype(BF16), (((0,), (0,)), ((), ())),
                           preferred_element_type=F32)


def _split(x, n):
    parts = []
    rem = x
    for _ in range(n):
        p = rem.astype(BF16)
        parts.append(p)
        rem = rem - p.astype(F32)
    return parts


def _xdot_l(m01, x, n=2):
    return sum(jnp.dot(m01, p, preferred_element_type=F32) for p in _split(x, n))


def _xdot_r(x, m01, n=2):
    return sum(jnp.dot(p, m01, preferred_element_type=F32) for p in _split(x, n))


def _sigmoid(x):
    return 1.0 / (1.0 + jnp.exp(-x))


def _rms(x, g):
    return x * lax.rsqrt(jnp.mean(x * x, axis=-1, keepdims=True) + NORM_EPS) * g


def _tile_rows(x, reps):
    return jnp.concatenate([x] * reps, axis=0)


def _shift_rows(x, prev8, k):
    rolled = pltpu.roll(x, k, 0)
    head = jnp.where(_iota((8, 1), 0) < k, pltpu.roll(prev8, k, 0), rolled[0:8])
    return jnp.concatenate([head, rolled[8:]], axis=0)


def _const_spec(shape):
    return pl.BlockSpec(shape, lambda *_: (0,) * len(shape))


def _params(n_grid):
    return pltpu.CompilerParams(dimension_semantics=("arbitrary",) * n_grid,
                                vmem_limit_bytes=VMEM_LIMIT)


def _project_tile(x_ref, g_ref, w_ref, p_scr):
    @pl.when(pl.program_id(0) == 0)
    def _():
        p_scr[...] = jnp.zeros_like(p_scr)

    p_prev = p_scr[...]
    h = _rms(x_ref[...], g_ref[...]).astype(BF16)

    def emit(i):
        cols = slice(i * MXU_TILE, min((i + 1) * MXU_TILE, w_ref.shape[1]))
        p_scr[:, cols] = jnp.dot(h, w_ref[:, cols], preferred_element_type=F32)

    return p_prev, emit


def _first_tile_of_row(nt):
    return lax.rem(jnp.maximum(pl.program_id(0) - 1, 0), nt) == 0


def _mixer_call(kernel_fn, x2d, g, w, consts, out_widths, tc, nt, scratch, name, time_tables=()):
    n = x2d.shape[0]
    total = n // tc
    once = pl.Buffered(1)
    in_specs = [pl.BlockSpec((tc, D_MODEL), lambda s: (jnp.minimum(s, total - 1), 0)),
                _const_spec((1, D_MODEL)),
                pl.BlockSpec(w.shape, lambda s: (0, 0), pipeline_mode=once)]
    in_specs += [pl.BlockSpec((tc, t.shape[1]), lambda s: (lax.rem(jnp.maximum(s - 1, 0), nt), 0))
                 for t in time_tables]
    in_specs += [_const_spec(c.shape) for c in consts]
    return pl.pallas_call(
        kernel_fn, grid=(total + 1,), in_specs=in_specs,
        out_specs=[pl.BlockSpec((tc, ow), lambda s: (jnp.maximum(s - 1, 0), 0)) for ow in out_widths],
        out_shape=[jax.ShapeDtypeStruct((n, ow), BF16) for ow in out_widths],
        scratch_shapes=[pltpu.VMEM((tc, w.shape[1]), F32)] + scratch,
        compiler_params=_params(1), name=name,
    )(x2d, g.reshape(1, D_MODEL), w, *time_tables, *consts)


def _mix_mlp_kernel(*refs, n_y, final):
    y_refs, rest = refs[:n_y], refs[n_y:]
    x_ref, g_ref, gf_ref = rest[0], rest[1], rest[2]
    wo_refs, (w1_ref, w2_ref, o_ref) = rest[3:3 + n_y], rest[3 + n_y:]
    x1 = x_ref[...]
    for y_ref, wo_ref in zip(y_refs, wo_refs):
        x1 = x1 + jnp.dot(y_ref[...], wo_ref[...], preferred_element_type=F32)
    h = _rms(x1, g_ref[...]).astype(BF16)
    u = jnp.dot(h, w1_ref[...], preferred_element_type=F32)
    u = jnp.square(jnp.maximum(u, 0.0)).astype(BF16)
    x2 = x1 + jnp.dot(u, w2_ref[...], preferred_element_type=F32)
    if final:
        x2 = _rms(x2, gf_ref[...])
    o_ref[...] = x2


def _mix_mlp(ys, x2d, g, gf, w_outs, w1, w2, tm, final):
    n = x2d.shape[0]
    once = pl.Buffered(1)
    in_specs = [pl.BlockSpec((tm, y.shape[1]), lambda i: (i, 0)) for y in ys]
    in_specs += [pl.BlockSpec((tm, D_MODEL), lambda i: (i, 0)),
                 _const_spec((1, D_MODEL)), _const_spec((1, D_MODEL))]
    in_specs += [pl.BlockSpec(w.shape, lambda i: (0, 0), pipeline_mode=once) for w in w_outs]
    in_specs += [pl.BlockSpec(w1.shape, lambda i: (0, 0), pipeline_mode=once),
                 pl.BlockSpec(w2.shape, lambda i: (0, 0), pipeline_mode=once)]
    return pl.pallas_call(
        functools.partial(_mix_mlp_kernel, n_y=len(ys), final=final),
        grid=(n // tm,), in_specs=in_specs,
        out_specs=pl.BlockSpec((tm, D_MODEL), lambda i: (i, 0)),
        out_shape=jax.ShapeDtypeStruct((n, D_MODEL), F32),
        compiler_params=_params(1), name="mix_mlp",
    )(*ys, x2d, g.reshape(1, D_MODEL), gf.reshape(1, D_MODEL), *w_outs, w1, w2)


def _layer0_kernel(x_ref, g_ref, w_ref, cos_ref, sin_ref, mu_ref, wa_ref, gup_ref, vec_ref, o_ref, ob_ref,
                   p_scr, prev_scr, st_scr, y_scr, rst_scr, ro_scr, *, tc, nt):
    T = RWKV_CHUNK
    G = MXU_TILE
    n_groups = RWKV_WIDTH // G
    n_chunks = tc // T

    @pl.when(_first_tile_of_row(nt))
    def _():
        prev_scr[...] = jnp.zeros_like(prev_scr)
        st_scr[...] = jnp.zeros_like(st_scr)
        rst_scr[...] = jnp.zeros_like(rst_scr)

    n_proj = RWKV_COLS // MXU_TILE
    p_all, project = _project_tile(x_ref, g_ref, w_ref, p_scr)
    p_raw = p_all[:, :RWKV_COLS]
    ret = _retention_parts(p_all[:, RWKV_COLS:], cos_ref, sin_ref, ob_ref, rst_scr, ro_scr, tc)
    ret_blocks = list(range(n_proj, (RWKV_COLS + RET_COLS) // MXU_TILE))

    def fill(n):
        for _ in range(min(n, len(ret_blocks))):
            project(ret_blocks.pop(0))
    shifted = _shift_rows(p_raw, prev_scr[...], 1)
    prev_scr[...] = p_raw[tc - 8:tc]
    p = p_raw + (shifted - p_raw) * mu_ref[...]

    w0, a0, k_k, k_a = (vec_ref[i:i + 1] for i in range(4))
    r_k, ln_w, ln_b = (vec_ref[i:i + 1] for i in range(4, 7))
    W = RWKV_WIDTH
    r, k, v = p[:, 0:W], p[:, W:2 * W], p[:, 2 * W:3 * W]
    lora = p[:, 3 * W:3 * W + RWKV_LORA]
    g_lo = p[:, 3 * W + RWKV_LORA:]

    lora = jnp.where(_iota((1, RWKV_LORA), 1) < RWKV_LORA // 2, jnp.tanh(lora), lora)
    za = _bdot(lora, wa_ref[...])
    ld = (-math.exp(-0.5) * LOG2E) * _sigmoid(w0 + za[:, :W])
    a = _sigmoid(a0 + za[:, W:])
    g = _bdot(_sigmoid(g_lo), gup_ref[...])
    project(0)
    fill(2)
    ret["scores"]()

    ones_g = (_blk((G, G), 0, RWKV_HEAD) == _blk((G, G), 1, RWKV_HEAD)).astype(BF16)

    def head_sum(x):
        return jnp.concatenate([_bdot(x[:, i * G:(i + 1) * G], ones_g) for i in range(n_groups)], axis=1)

    kk = k * k_k
    kk = kk * lax.rsqrt(jnp.maximum(head_sum(kk * kk), 1e-24))
    project(1)
    fill(2)
    k = k * (1.0 + (a - 1.0) * k_a)
    a_s = -kk
    b_s = kk * a
    bonus = head_sum(r * k * r_k) * v
    project(2)
    fill(2)
    ret["kv"]()

    ri, ci = _iota((G, G), 0), _iota((G, G), 1)
    tri_g = ((_blk((G, G), 0, T) == _blk((G, G), 1, T)) & (ci <= ri)).astype(BF16)
    L = jnp.concatenate([_xdot_l(tri_g, ld[i:i + G]) for i in range(0, tc, G)], axis=0)
    project(3)
    fill(2)
    LT = jnp.concatenate([jnp.broadcast_to(L[(c + 1) * T - 1:(c + 1) * T], (T, W))
                          for c in range(n_chunks)], axis=0)
    e_neg = jnp.exp2(-L)
    e_end = jnp.exp2(LT - L)
    rt = r * jnp.exp2(L)
    at = a_s * jnp.exp2(L - ld)
    bt, kt = b_s * e_neg, k * e_neg
    bh, kh = b_s * e_end, k * e_end
    w_end = jnp.exp2(LT)

    row, col = _iota((T, G), 0), _off((T, G), 1, T)
    strict, incl = col < row, col <= row
    eye_cat = (col == row).astype(F32)
    bd_mask = _blk((G, G), 0, T) == _blk((G, G), 1, T)
    eye_g = _iota((G, G), 0) == _iota((G, G), 1)

    def bd(x):
        return jnp.where(bd_mask, _tile_rows(x.astype(BF16), G // T), jnp.zeros((), BF16))

    units = [(gi, c) for c in range(n_chunks) for gi in range(n_groups)]
    sl = {u: (slice(u[1] * T, (u[1] + 1) * T), slice(u[0] * G, (u[0] + 1) * G)) for u in units}
    lhs = {u: jnp.concatenate([at[sl[u]], rt[sl[u]]], axis=0) for u in units}
    a1 = {u: _bdot_nt(lhs[u], bd(bt[sl[u]])) for u in units}
    a2 = {u: _bdot_nt(lhs[u], bd(kt[sl[u]])) for u in units}
    a_ab = {u: jnp.where(strict, a1[u][:T], 0.0) for u in units}
    a_rb = {u: jnp.where(incl, a1[u][T:], 0.0) for u in units}
    a_ak = {u: jnp.where(strict, a2[u][:T], 0.0) for u in units}
    a_rk = {u: jnp.where(incl, a2[u][T:], 0.0) for u in units}
    acc = {u: eye_cat + a_ab[u] for u in units}
    pw = {u: _bdot(a_ab[u], bd(a_ab[u])) for u in units}
    av = {u: _bdot(a_ak[u], bd(v[sl[u]])) for u in units}
    for _ in range(int(math.log2(T)) - 2):
        both = {u: _bdot(jnp.concatenate([pw[u], acc[u]], axis=0), bd(pw[u])) for u in units}
        pw = {u: both[u][:T] for u in units}
        acc = {u: acc[u] + both[u][T:] for u in units}
    tinv = {u: acc[u] + _bdot(acc[u], bd(pw[u])) for u in units}
    pu = {u: _bdot(tinv[u], jnp.concatenate([bd(at[sl[u]]), bd(av[u])], axis=1)) for u in units}
    p_c = {u: pu[u][:, :G] for u in units}
    u0 = {u: pu[u][:, G:] for u in units}
    half = G // 2
    zero_pair = jnp.zeros((T, half), BF16)
    m_bd, n_bd = {}, {}
    for u in units:
        mn = []
        for pr in range(2):
            ps = slice(pr * half, (pr + 1) * half)
            lhs_p = jnp.concatenate([bh[sl[u]][:, ps], kh[sl[u]][:, ps]], axis=0).astype(BF16)
            rhs_p = jnp.concatenate(
                [jnp.concatenate([p_c[u][:, ps].astype(BF16), u0[u][:, ps].astype(BF16)], axis=1),
                 jnp.concatenate([zero_pair, v[sl[u]][:, ps].astype(BF16)], axis=1)], axis=0)
            mn.append(_bdot_tn(lhs_p, rhs_p))
        zeros_h = jnp.zeros((half, half), F32)
        m_full = jnp.concatenate([jnp.concatenate([mn[0][:, :half], zeros_h], axis=1),
                                  jnp.concatenate([zeros_h, mn[1][:, :half]], axis=1)], axis=0)
        n_full = jnp.concatenate([jnp.concatenate([mn[0][:, half:], zeros_h], axis=1),
                                  jnp.concatenate([zeros_h, mn[1][:, half:]], axis=1)], axis=0)
        m_bd[u] = (jnp.where(bd_mask, m_full, 0.0)
                   + jnp.where(eye_g, w_end[u[1] * T:u[1] * T + 1, sl[u][1]], 0.0))
        n_bd[u] = jnp.where(bd_mask, n_full, 0.0)
    q_c = {u: rt[sl[u]] + _bdot(a_rb[u], bd(p_c[u])) for u in units}
    y0 = {u: _bdot(jnp.concatenate([a_rb[u], a_rk[u]], axis=1),
                   jnp.concatenate([bd(u0[u]), bd(v[sl[u]])], axis=0)) for u in units}
    st = [st_scr[gi] for gi in range(n_groups)]
    blocks = list(range(4, n_proj))
    for u in units:
        gi, c = u
        y_scr[sl[u]] = _bdot(q_c[u], st[gi]) + y0[u]
        st[gi] = _bdot(m_bd[u], st[gi]) + n_bd[u]
        if gi == n_groups - 1 and c + 1 < n_chunks:
            for _ in range(-(-len(blocks) // (n_chunks - 1 - c))):
                project(blocks.pop(0))
            fill(1)
    for gi in range(n_groups):
        st_scr[gi] = st[gi]
    ret["out"]()
    fill(8)

    y = y_scr[...]
    mean = head_sum(y) * (1.0 / RWKV_HEAD)
    yc = y - mean
    var = head_sum(yc * yc) * (1.0 / RWKV_HEAD)
    y = yc * lax.rsqrt(var + RWKV_GN_EPS) * ln_w + ln_b
    o_ref[...] = ((y + bonus) * g).astype(o_ref.dtype)
    ret["post"]()


def _layer0(x2d, g, w, cos, sin, mu, wa, gup, vec, tc, nt):
    scratch = [pltpu.VMEM((8, RWKV_COLS), F32),
               pltpu.VMEM((RWKV_WIDTH // MXU_TILE, MXU_TILE, MXU_TILE), F32),
               pltpu.VMEM((tc, RWKV_WIDTH), F32),
               pltpu.VMEM((RET_WIDTH // MXU_TILE, MXU_TILE, MXU_TILE), F32),
               pltpu.VMEM((tc, RET_WIDTH), F32)]
    return _mixer_call(functools.partial(_layer0_kernel, tc=tc, nt=nt), x2d, g, w, [mu, wa, gup, vec],
                       [RWKV_WIDTH, RET_WIDTH], tc, nt, scratch, "rwkv7_retention", time_tables=[cos, sin])


def _retention_parts(p, cos_ref, sin_ref, o_ref, st_scr, o_scr, tc):
    L = CHUNK
    G = MXU_TILE
    W = RET_WIDTH
    n_groups = W // G
    hpg = G // RET_HEAD
    cos, sin = cos_ref[...], sin_ref[...]

    def rope(x):
        return jnp.concatenate(
            [x[:, h * RET_HEAD:(h + 1) * RET_HEAD] * cos
             + pltpu.roll(x[:, h * RET_HEAD:(h + 1) * RET_HEAD], RET_HEAD // 2, 1) * sin
             for h in range(RET_HEADS)], axis=1)

    def lane_gamma(gi):
        lg = [math.log1p(-2.0 ** (-5.0 - (gi * hpg + h))) for h in range(hpg)]
        out = jnp.full((1, G), lg[-1], F32)
        for h in range(hpg - 1, -1, -1):
            out = jnp.where(_iota((1, G), 1) < (h + 1) * RET_HEAD, lg[h], out)
        return out

    rel = (_iota((L, G), 0) - _off((L, G), 1, L)).astype(F32)
    pos = _iota((L, G), 0).astype(F32)
    bd_mask = _blk((G, G), 0, RET_HEAD) == _blk((G, G), 1, RET_HEAD)

    def bd(x):
        return jnp.where(bd_mask, _tile_rows(x.astype(BF16), hpg), jnp.zeros((), BF16))

    units = [(gi, c) for c in range(tc // L) for gi in range(n_groups)]
    sl = {u: (slice(u[1] * L, (u[1] + 1) * L), slice(u[0] * G, (u[0] + 1) * G)) for u in units}
    lg = {gi: lane_gamma(gi) for gi in range(n_groups)}
    val = {}

    def scores():
        q = rope(p[:, 0:W])
        k = rope(p[:, W:2 * W]) * (RET_HEAD ** -0.5)
        val.update(q=q, k=k, v=p[:, 2 * W:3 * W])
        decay = {gi: jnp.where(rel >= 0, jnp.exp(lg[gi] * jnp.maximum(rel, 0.0)), 0.0) for gi in range(n_groups)}
        val["s"] = {u: _bdot_nt(q[sl[u]], bd(k[sl[u]])) * decay[u[0]] for u in units}

    def kv():
        k, v = val["k"], val["v"]
        k_dec = {gi: jnp.exp(lg[gi] * (L - 1.0 - pos)) for gi in range(n_groups)}
        val["kv"] = {u: jnp.where(bd_mask, _bdot_tn(k[sl[u]] * k_dec[u[0]], v[sl[u]]), 0.0) for u in units}

    def out():
        q, v = val["q"], val["v"]
        st = [st_scr[gi] for gi in range(n_groups)]
        for u in units:
            gi = u[0]
            q_dec = jnp.exp(lg[gi] * (pos + 1.0))
            c_dec = jnp.exp(lg[gi] * float(L))
            c_rows = jnp.concatenate([jnp.broadcast_to(c_dec[:, h * RET_HEAD:h * RET_HEAD + 1], (RET_HEAD, 1))
                                      for h in range(hpg)], axis=0)
            o_scr[sl[u]] = _bdot(val["s"][u], bd(v[sl[u]])) + _bdot(q[sl[u]], st[gi]) * q_dec
            st[gi] = st[gi] * c_rows + val["kv"][u]
        for gi in range(n_groups):
            st_scr[gi] = st[gi]

    def post():
        o = o_scr[...]
        gate = p[:, 3 * W:]
        outs = []
        for h in range(RET_HEADS):
            oh = o[:, h * RET_HEAD:(h + 1) * RET_HEAD]
            outs.append(oh * lax.rsqrt(jnp.mean(oh * oh, axis=-1, keepdims=True) + NORM_EPS))
        o_ref[...] = (jnp.concatenate(outs, axis=1) * (gate * _sigmoid(gate))).astype(o_ref.dtype)

    return dict(scores=scores, kv=kv, out=out, post=post)


def _mlstm_kernel(x_ref, g_ref, w_ref, cw_ref, vec_ref, gb_ref, o_ref,
                  p_scr, prev_scr, c_scr, m_scr, h_scr, *, tc, nt):
    L = CHUNK
    G = MXU_TILE
    hpg = G // ML_V
    GK = hpg * ML_QK
    n_groups = ML_HEADS // hpg
    n_chunks = tc // L
    QW = ML_HEADS * ML_QK

    @pl.when(_first_tile_of_row(nt))
    def _():
        prev_scr[...] = jnp.zeros_like(prev_scr)
        c_scr[...] = jnp.zeros_like(c_scr)
        m_scr[...] = jnp.zeros_like(m_scr)

    p, project = _project_tile(x_ref, g_ref, w_ref, p_scr)

    def project_blocks(*blocks):
        for i in blocks:
            project(i)
    D = D_MODEL
    conv_b, norm_w = vec_ref[0:1], vec_ref[1:2]
    x = p[:, :D]
    prev8 = prev_scr[...]
    prev_scr[...] = x[tc - 8:tc]
    acc = x * cw_ref[ML_CONV - 1:ML_CONV] + conv_b
    for j in range(ML_CONV - 1):
        acc = acc + _shift_rows(x, prev8, ML_CONV - 1 - j) * cw_ref[j:j + 1]
    qk = acc * _sigmoid(acc)
    q, k = qk[:, :QW], qk[:, QW:] * (ML_QK ** -0.5)
    v = p[:, D:2 * D]
    project_blocks(0, 1, 2)

    pre = p[:, 3 * D:] + gb_ref[...]
    cap = GATE_SOFTCAP * jnp.tanh(pre * (1.0 / GATE_SOFTCAP))
    li_all = cap
    lf_all = pltpu.roll(jnp.minimum(cap, 0.0) - jnp.log(1.0 + jnp.exp(-jnp.abs(cap))),
                        GATE_PAD - ML_HEADS, 1)

    tri = (_iota((L, L), 1) <= _iota((L, L), 0)).astype(BF16)
    causal = _iota((L, L), 1) <= _iota((L, L), 0)
    rows = _iota((L, 1), 0)
    k_mask = _blk((G, GK), 0, L) == _blk((G, GK), 1, ML_QK)
    v_mask = _blk((G, G), 0, L) == _blk((G, G), 1, ML_V)
    c_mask = _blk((GK, G), 0, ML_QK) == _blk((GK, G), 1, ML_V)
    c_mask2 = jnp.concatenate([c_mask, c_mask], axis=1)
    ones_bd = v_mask.astype(BF16)
    lane_k = _iota((1, GK), 1)
    row_k = _iota((GK, 1), 0)

    units = [(c, gi) for c in range(n_chunks) for gi in range(n_groups)]
    rs = {c: slice(c * L, (c + 1) * L) for c in range(n_chunks)}
    kl = {gi: slice(gi * GK, (gi + 1) * GK) for gi in range(n_groups)}
    vl = {gi: slice(gi * G, (gi + 1) * G) for gi in range(n_groups)}

    s_raw = {}
    v_bd = {}
    for (c, gi) in units:
        k_bd = jnp.where(k_mask, _tile_rows(k[rs[c], kl[gi]].astype(BF16), hpg), jnp.zeros((), BF16))
        v_bd[c, gi] = jnp.concatenate(
            [jnp.where(v_mask, _tile_rows(v[rs[c], vl[gi]].astype(BF16), hpg), jnp.zeros((), BF16)), ones_bd],
            axis=1)
        s_raw[c, gi] = _bdot_nt(q[rs[c], kl[gi]], k_bd)

    project_blocks(3, 4, 5)

    m_row = m_scr[0:1, :]
    gate = {}
    for c in range(n_chunks):
        li = li_all[rs[c]]
        b = _xdot_l(tri, lf_all[rs[c]], 3)
        beta = li - b
        cm = beta
        sh = 1
        while sh < L:
            cm = jnp.maximum(cm, jnp.where(rows >= sh, pltpu.roll(cm, sh, 0), -jnp.inf))
            sh *= 2
        mu = jnp.maximum(cm, m_row)
        mu_end = mu[L - 1:L]
        gate[c] = dict(beta_t=(beta * LOG2E).T, mu2=mu * LOG2E, inter=jnp.exp(m_row - mu),
                       emt=jnp.exp(-(b + mu)), kwf=jnp.exp(beta - mu_end))
        m_row = b[L - 1:L] + mu_end
    m_scr[0:1, :] = m_row
    project_blocks(6, 7, 8)

    s_dec, num, kv = {}, {}, {}
    for (c, gi) in units:
        gt = gate[c]
        dms = []
        for h in range(hpg):
            j = gi * hpg + h
            dms.append(jnp.where(causal, jnp.exp2(gt["beta_t"][j:j + 1, :] - gt["mu2"][:, j:j + 1]), 0.0))
        s_dec[c, gi] = s_raw[c, gi] * jnp.concatenate(dms, axis=1)
        kwf = gt["kwf"][:, gi * hpg + hpg - 1:gi * hpg + hpg]
        for h in range(hpg - 2, -1, -1):
            kwf = jnp.where(lane_k < (h + 1) * ML_QK, gt["kwf"][:, gi * hpg + h:gi * hpg + h + 1], kwf)
        kw = k[rs[c], kl[gi]] * kwf
        num[c, gi] = _bdot(s_dec[c, gi], v_bd[c, gi])
        v_one = jnp.concatenate([v[rs[c], vl[gi]].astype(BF16), jnp.ones((L, G), BF16)], axis=1)
        kv[c, gi] = jnp.where(c_mask2, _bdot_tn(kw, v_one), 0.0)

    project_blocks(9, 10)

    c_st = [c_scr[gi] for gi in range(n_groups)]
    for (c, gi) in units:
        gt = gate[c]
        q_g = q[rs[c], kl[gi]]
        qc = _bdot(q_g, c_st[gi])
        cs = [gt["inter"][L - 1:L, gi * hpg + h:gi * hpg + h + 1] for h in range(hpg)]
        inter = jnp.concatenate([jnp.broadcast_to(gt["inter"][:, gi * hpg + h:gi * hpg + h + 1], (L, ML_V))
                                 for h in range(hpg)], axis=1)
        emt = jnp.concatenate([jnp.broadcast_to(gt["emt"][:, gi * hpg + h:gi * hpg + h + 1], (L, ML_V))
                               for h in range(hpg)], axis=1)
        den = num[c, gi][:, G:] + inter * qc[:, G:]
        rden = 1.0 / jnp.maximum(jnp.abs(den), emt)
        h_scr[rs[c], vl[gi]] = (num[c, gi][:, :G] + inter * qc[:, :G]) * rden
        cs_rows = cs[hpg - 1]
        for h in range(hpg - 2, -1, -1):
            cs_rows = jnp.where(row_k < (h + 1) * ML_QK, cs[h], cs_rows)
        c_st[gi] = cs_rows * c_st[gi] + kv[c, gi]
    for gi in range(n_groups):
        c_scr[gi] = c_st[gi]
    project_blocks(11, 12)

    hfull = h_scr[...]
    outs = []
    for h in range(ML_HEADS):
        hh = hfull[:, h * ML_V:(h + 1) * ML_V]
        outs.append(hh * lax.rsqrt(jnp.mean(hh * hh, axis=-1, keepdims=True) + NORM_EPS))
    hn = jnp.concatenate(outs, axis=1) * norm_w
    o_ref[...] = (hn * _sigmoid(p[:, 2 * D:3 * D])).astype(o_ref.dtype)


def _mlstm(x2d, g, w, conv_w, vec, gbias, tc, nt):
    hpg = MXU_TILE // ML_V
    scratch = [pltpu.VMEM((8, D_MODEL), F32),
               pltpu.VMEM((ML_HEADS // hpg, hpg * ML_QK, 2 * MXU_TILE), F32),
               pltpu.VMEM((8, LANES), F32),
               pltpu.VMEM((tc, D_MODEL), F32)]
    return _mixer_call(functools.partial(_mlstm_kernel, tc=tc, nt=nt), x2d, g, w, [conv_w, vec, gbias],
                       [D_MODEL], tc, nt, scratch, "mlstm")[0]


def kernel(x, norm_mix_g, norm_mlp_g, norm_final_g, ab_w_in, rwkv_mu, rwkv_w0, rwkv_w_up, rwkv_a0,
           rwkv_a_up, rwkv_g_up, rwkv_k_k, rwkv_k_a, rwkv_r_k, rwkv_ln_w, rwkv_ln_b, ab_w_out,
           c_w_in, c_conv_w, c_conv_b, c_i_bias, c_f_bias, c_norm_w, c_w_out, mlp_w1, mlp_w2):
    bsz, seq, d = x.shape
    n = bsz * seq
    tm = MLP_TILE
    bf = lambda w: w.astype(BF16)
    x2d = x.reshape(n, d)

    w_in = ab_w_in[0]
    half = RWKV_LORA // 2
    zeros = jnp.zeros((half, RWKV_WIDTH), F32)
    wa = bf(jnp.concatenate([jnp.concatenate([rwkv_w_up[0], zeros], axis=1),
                             jnp.concatenate([zeros, rwkv_a_up[0]], axis=1)], axis=0))
    vec = jnp.stack([rwkv_w0[0], rwkv_a0[0], rwkv_k_k[0], rwkv_k_a[0], rwkv_r_k[0].reshape(-1),
                     rwkv_ln_w[0], rwkv_ln_b[0], jnp.zeros((RWKV_WIDTH,), F32)])
    pos = jnp.arange(seq, dtype=F32)
    inv = ROPE_BASE ** (-jnp.arange(0, RET_HEAD, 2, dtype=F32) / RET_HEAD)
    ang = pos[:, None] * inv[None, :]
    cos, sin = jnp.cos(ang), jnp.sin(ang)
    y_a, y_b = _layer0(x2d, norm_mix_g[0], bf(w_in), jnp.concatenate([cos, cos], axis=1),
                       jnp.concatenate([-sin, sin], axis=1), rwkv_mu[0].reshape(1, RWKV_COLS), wa,
                       bf(rwkv_g_up[0]), vec, RWKV_TILE, seq // RWKV_TILE)

    w_out = bf(ab_w_out[0])
    x2d = _mix_mlp([y_a, y_b], x2d, norm_mlp_g[0],
                   norm_final_g, [w_out[:RWKV_WIDTH], w_out[RWKV_WIDTH:]], bf(mlp_w1[0]), bf(mlp_w2[0]),
                   tm, final=False)

    w_in = jnp.pad(c_w_in[0], ((0, 0), (0, GATE_PAD - 2 * ML_HEADS)))
    gbias = jnp.pad(jnp.concatenate([c_i_bias[0], c_f_bias[0]]), (0, GATE_PAD - 2 * ML_HEADS)).reshape(1, GATE_PAD)
    vec = jnp.stack([c_conv_b[0], c_norm_w[0]] + [jnp.zeros((d,), F32)] * 6)
    y_c = _mlstm(x2d, norm_mix_g[1], bf(w_in), c_conv_w[0], vec, gbias, ML_TILE, seq // ML_TILE)

    x2d = _mix_mlp([y_c], x2d, norm_mlp_g[1], norm_final_g, [bf(c_w_out[0])],
                   bf(mlp_w1[1]), bf(mlp_w2[1]), tm, final=True)
    return x2d.reshape(bsz, seq, d)
```
